```python
import math
import jax, jax.numpy as jnp
from jax import lax
import numpy as np

D_MODEL = 4096
BATCH = 1
SEQ = 16384
DEPTH = 4

N_A_LAYERS = DEPTH // 2
N_B_LAYERS = DEPTH - N_A_LAYERS
D_FF = 6144
PLE_DIM = 256
ALPHA = (2.0 * DEPTH) ** 0.25
BETA = (8.0 * DEPTH) ** -0.25
LN_EPS = 1e-5

RW_HEAD = 64
RW_HEADS = D_MODEL // RW_HEAD
RW_DECAY_LORA = 128
RW_AAA_LORA = 128
RW_MV_LORA = 96
RW_GATE_LORA = 480
RW_GN_EPS = 64e-5

NSA_HEADS = 32
NSA_HEAD_DIM = D_MODEL // NSA_HEADS
NSA_KV_GROUPS = 4
NSA_HPG = NSA_HEADS // NSA_KV_GROUPS
CMP_BLOCK = 32
CMP_STRIDE = 16
CMP_HIDDEN = 1024
SEL_BLOCK = 64
SEL_TOP_N = 16
WINDOW = 512
Q_BLOCK = 128
FORCE_SCORE = 1e4
NEG_INF = -1e30

kernel_name = 'yoco_rwkv7_nsa_macaron_deepnorm'


def layer_norm(x, g, b):
    xf = x.astype(jnp.float32)
    mu = jnp.mean(xf, -1, keepdims=True)
    var = jnp.mean(jnp.square(xf - mu), -1, keepdims=True)
    return ((xf - mu) * lax.rsqrt(var + LN_EPS) * g + b).astype(x.dtype)


def swiglu(x, w13, w2):
    a, b = jnp.split(x @ w13, 2, axis=-1)
    return (jax.nn.silu(a) * b) @ w2


def token_shift(x):
    return jnp.pad(x, ((0, 0), (1, 0), (0, 0)))[:, :-1]


def alibi_slopes(n):
    return jnp.exp2(-8.0 * jnp.arange(1, n + 1, dtype=jnp.float32) / n)


def rwkv7_recurrence(r, w, k, v, a_vec, b_vec):
    B, T, H, N = r.shape

    def step(S, inp):
        r_t, w_t, k_t, v_t, a_t, b_t = inp
        sa = jnp.einsum('bhvk,bhk->bhv', S, a_t)
        S = S * w_t[:, :, None, :] + sa[..., None] * b_t[:, :, None, :] + v_t[..., None] * k_t[:, :, None, :]
        return S, jnp.einsum('bhvk,bhk->bhv', S, r_t)

    xs = tuple(jnp.moveaxis(z, 1, 0) for z in (r, w, k, v, a_vec, b_vec))
    _, y = lax.scan(step, jnp.zeros((B, H, N, N), jnp.float32), xs)
    return jnp.moveaxis(y, 0, 1)


def rwkv7_time_mix(x, v_first, mix, w_rkv, w_o, w0, w1, w2, a0, a1, a2, g1, g2,
                   k_k, k_a, r_k, lnx_g, lnx_b, v_lora):
    B, T, C = x.shape
    H, N = RW_HEADS, RW_HEAD
    xx = token_shift(x) - x
    xm = x[None] + xx[None] * mix[:, None, None, :]
    r, k, v = jnp.einsum('cbtd,cde->cbte', xm[:3], w_rkv)
    log_w = -jax.nn.softplus(-(w0 + jnp.tanh(xm[3] @ w1) @ w2)) - 0.5
    decay = jnp.exp(-jnp.exp(log_w.astype(jnp.float32)))
    if v_lora is None:
        v_first = v
    else:
        v0, v1, v2 = v_lora
        v = v + (v_first - v) * jax.nn.sigmoid(v0 + (xm[2] @ v1) @ v2)
    a = jax.nn.sigmoid(a0 + (xm[4] @ a1) @ a2)
    g = jax.nn.sigmoid(xm[5] @ g1) @ g2
    heads = lambda z: z.reshape(B, T, H, N).astype(jnp.float32)
    kk = heads(k * k_k)
    kk = kk * lax.rsqrt(jnp.maximum(jnp.sum(kk * kk, -1, keepdims=True), 1e-24))
    k = k * (1 + (a - 1) * k_a)
    rh, kh, vh, ah = heads(r), heads(k), heads(v), heads(a)
    y = rwkv7_recurrence(rh, heads(decay), kh, vh, -kk, kk * ah)
    mu = jnp.mean(y, -1, keepdims=True)
    var = jnp.mean(jnp.square(y - mu), -1, keepdims=True)
    yn = ((y - mu) * lax.rsqrt(var + RW_GN_EPS)).reshape(B, T, C) * lnx_g + lnx_b
    bonus = (jnp.sum(rh * kh * r_k, -1, keepdims=True) * vh).reshape(B, T, C)
    out = ((yn + bonus) * g).astype(x.dtype)
    return out @ w_o, v_first


def nsa_shared_kv(h, w_kv, cmp_pos, cmp_w1, cmp_b1, cmp_w2):
    B, T, _ = h.shape
    G, DK = NSA_KV_GROUPS, NSA_HEAD_DIM
    kv = jnp.transpose((h @ w_kv).reshape(B, T, 6, G, DK), (2, 0, 3, 1, 4))
    n_chunk = T // CMP_STRIDE
    n_ov = CMP_BLOCK // CMP_STRIDE
    n_cmp = n_chunk - n_ov + 1

    def compress(z, j):
        ch = z.reshape(B, G, n_chunk, CMP_STRIDE, DK)
        blocks = jnp.concatenate([ch[:, :, o:o + n_cmp] for o in range(n_ov)], axis=3)
        flat = (blocks + cmp_pos[j]).reshape(B, G, n_cmp, CMP_BLOCK * DK)
        return jax.nn.silu(flat @ cmp_w1[j] + cmp_b1[j]) @ cmp_w2[j]

    return (compress(kv[0], 0), compress(kv[1], 1), kv[2], kv[3], kv[4], kv[5])


def selection_scores(p_grp, n_sel):
    ratio = SEL_BLOCK // CMP_STRIDE
    n_ov = CMP_BLOCK // CMP_STRIDE
    padded = jnp.pad(p_grp, ((0, 0), (0, 0), (0, 0), (n_ov - 1, n_ov - 1)))
    stop = ratio * (n_sel - 1) + 1
    out = jnp.zeros(p_grp.shape[:3] + (n_sel,), p_grp.dtype)
    for m in range(ratio):
        for n in range(n_ov):
            s0 = m - n + n_ov - 1
            out = out + padded[..., s0:s0 + stop:ratio]
    return out


def masked_softmax(s, valid):
    p = jax.nn.softmax(jnp.where(valid, s, NEG_INF), axis=-1)
    return jnp.where(valid, p, 0.0)


def nsa_attention(x, k_cmp, v_cmp, k_slc, v_slc, k_win, v_win, w_qg, b_g, w_o):
    B, T, C = x.shape
    H, G, HPG, DK = NSA_HEADS, NSA_KV_GROUPS, NSA_HPG, NSA_HEAD_DIM
    n_qb = T // Q_BLOCK
    n_sel = T // SEL_BLOCK
    n_top = min(SEL_TOP_N, n_sel)
    n_cmp = k_cmp.shape[2]
    scale = DK ** -0.5
    qg = x @ w_qg
    q = qg[..., :H * DK].reshape(B, n_qb, Q_BLOCK, G, HPG, DK).transpose(1, 0, 3, 4, 2, 5)
    gate = jax.nn.sigmoid(qg[..., H * DK:] + b_g).reshape(B, n_qb, Q_BLOCK, G, HPG, 3).transpose(1, 0, 3, 4, 2, 5)
    slope = alibi_slopes(H).reshape(G, HPG, 1, 1)
    pos_cmp = jnp.arange(n_cmp) * CMP_STRIDE + (CMP_BLOCK - 1)
    ks_blocks = k_slc.reshape(B, G, n_sel, SEL_BLOCK, DK)
    vs_blocks = v_slc.reshape(B, G, n_sel, SEL_BLOCK, DK)
    k_win_pad = jnp.pad(k_win, ((0, 0), (0, 0), (WINDOW, 0), (0, 0)))
    v_win_pad = jnp.pad(v_win, ((0, 0), (0, 0), (WINDOW, 0), (0, 0)))
    sel_j = jnp.arange(n_sel)
    in_block = jnp.arange(SEL_BLOCK)
    win_off = jnp.arange(Q_BLOCK + WINDOW) - WINDOW
    gather = jax.vmap(jax.vmap(lambda blocks, ix: blocks[ix]))

    def block(args):
        qb, q_b, g_b = args
        t = qb * Q_BLOCK + jnp.arange(Q_BLOCK)
        d_c = t[:, None] - pos_cmp[None, :]
        s_c = jnp.einsum('bghqd,bgcd->bghqc', q_b, k_cmp).astype(jnp.float32) * scale - slope * d_c.astype(jnp.float32)
        p_c = masked_softmax(s_c, d_c >= 0)
        o_c = jnp.einsum('bghqc,bgcd->bghqd', p_c.astype(v_cmp.dtype), v_cmp)
        imp = selection_scores(jnp.sum(p_c, axis=2), n_sel)
        cur = (t // SEL_BLOCK)[:, None]
        forced = (sel_j == 0) | (sel_j == cur) | (sel_j == cur - 1)
        imp = jnp.where(sel_j > cur, -1.0, jnp.where(forced, FORCE_SCORE, imp))
        _, idx = lax.top_k(imp, n_top)
        k_s = gather(ks_blocks, idx).reshape(B, G, Q_BLOCK, n_top * SEL_BLOCK, DK)
        v_s = gather(vs_blocks, idx).reshape(B, G, Q_BLOCK, n_top * SEL_BLOCK, DK)
        pos_s = (idx[..., None] * SEL_BLOCK + in_block).reshape(B, G, Q_BLOCK, n_top * SEL_BLOCK)
        d_s = (t[:, None] - pos_s)[:, :, None]
        s_s = jnp.einsum('bghqd,bgqkd->bghqk', q_b, k_s).astype(jnp.float32) * scale - slope * d_s.astype(jnp.float32)
        p_s = masked_softmax(s_s, d_s >= 0)
        o_s = jnp.einsum('bghqk,bgqkd->bghqd', p_s.astype(v_s.dtype), v_s)
        start = qb * Q_BLOCK
        k_w = lax.dynamic_slice_in_dim(k_win_pad, start, Q_BLOCK + WINDOW, axis=2)
        v_w = lax.dynamic_slice_in_dim(v_win_pad, start, Q_BLOCK + WINDOW, axis=2)
        pos_w = start + win_off
        d_w = t[:, None] - pos_w[None, :]
        valid_w = (d_w >= 0) & (d_w < WINDOW) & (pos_w >= 0)[None, :]
        s_w = jnp.einsum('bghqd,bgkd->bghqk', q_b, k_w).astype(jnp.float32) * scale - slope * d_w.astype(jnp.float32)
        p_w = masked_softmax(s_w, valid_w)
        o_w = jnp.einsum('bghqk,bgkd->bghqd', p_w.astype(v_w.dtype), v_w)
        return g_b[..., 0:1] * o_c + g_b[..., 1:2] * o_s + g_b[..., 2:3] * o_w

    o = lax.map(block, (jnp.arange(n_qb), q, gate))
    o = o.transpose(1, 0, 4, 2, 3, 5).reshape(B, T, C).astype(x.dtype)
    return o @ w_o


def setup_inputs(seed: int = 0) -> dict:
    key = jax.random.key(seed)
    keys = iter(list(jax.random.split(key, 48)))
    nrm = lambda shape, scale: jax.random.normal(next(keys), shape, jnp.float32) * scale
    uni = lambda shape, lo, hi: jax.random.uniform(next(keys), shape, jnp.float32, lo, hi)
    D, F, L = D_MODEL, D_FF, DEPTH
    NA, NB = N_A_LAYERS, N_B_LAYERS
    H, G, DK = NSA_HEADS, NSA_KV_GROUPS, NSA_HEAD_DIM
    return {
        'x': nrm((BATCH, SEQ, D), 1.0),
        'p': nrm((DEPTH, BATCH, SEQ, PLE_DIM), 1.0),
        'ln_g': 1.0 + nrm((L, 4, D), 0.02),
        'ln_b': nrm((L, 4, D), 0.02),
        'ffn1_w13': nrm((L, D, 2 * F), D ** -0.5),
        'ffn1_w2': nrm((L, F, D), BETA * F ** -0.5),
        'ffn2_w13': nrm((L, D, 2 * F), D ** -0.5),
        'ffn2_w2': nrm((L, F, D), BETA * F ** -0.5),
        'ple_w': nrm((L, PLE_DIM, D), BETA * PLE_DIM ** -0.5),
        'ple_gate_down': nrm((L, D, PLE_DIM), D ** -0.5),
        'ple_gate_up': nrm((L, PLE_DIM, D), PLE_DIM ** -0.5),
        'rw_mix': uni((NA, 6, D), 0.0, 1.0),
        'rw_w_rkv': nrm((NA, 3, D, D), D ** -0.5),
        'rw_w_o': nrm((NA, D, D), BETA * D ** -0.5),
        'rw_w0': uni((NA, D), -3.0, 0.5),
        'rw_w1': nrm((NA, D, RW_DECAY_LORA), D ** -0.5),
        'rw_w2': nrm((NA, RW_DECAY_LORA, D), 0.3 * RW_DECAY_LORA ** -0.5),
        'rw_a0': nrm((NA, D), 0.1),
        'rw_a1': nrm((NA, D, RW_AAA_LORA), D ** -0.5),
        'rw_a2': nrm((NA, RW_AAA_LORA, D), RW_AAA_LORA ** -0.5),
        'rw_g1': nrm((NA, D, RW_GATE_LORA), D ** -0.5),
        'rw_g2': nrm((NA, RW_GATE_LORA, D), RW_GATE_LORA ** -0.5),
        'rw_k_k': 0.85 + nrm((NA, D), 0.05),
        'rw_k_a': 1.0 + nrm((NA, D), 0.05),
        'rw_r_k': nrm((NA, RW_HEADS, RW_HEAD), 0.1),
        'rw_lnx_g': 1.0 + nrm((NA, D), 0.02),
        'rw_lnx_b': nrm((NA, D), 0.02),
        'rw_v0': 1.0 + nrm((NA - 1, D), 0.1),
        'rw_v1': nrm((NA - 1, D, RW_MV_LORA), D ** -0.5),
        'rw_v2': nrm((NA - 1, RW_MV_LORA, D), RW_MV_LORA ** -0.5),
        'nsa_w_kv': nrm((D, 6 * G * DK), D ** -0.5),
        'cmp_pos': nrm((2, CMP_BLOCK, DK), 0.1),
        'cmp_w1': nrm((2, CMP_BLOCK * DK, CMP_HIDDEN), (CMP_BLOCK * DK) ** -0.5),
        'cmp_b1': nrm((2, CMP_HIDDEN), 0.02),
        'cmp_w2': nrm((2, CMP_HIDDEN, DK), CMP_HIDDEN ** -0.5),
        'nsa_w_qg': nrm((NB, D, H * DK + 3 * H), D ** -0.5),
        'nsa_b_g': nrm((NB, 3 * H), 0.1),
        'nsa_w_o': nrm((NB, D, D), BETA * D ** -0.5),
    }


def reference(x, p, ln_g, ln_b, ffn1_w13, ffn1_w2, ffn2_w13, ffn2_w2, ple_w, ple_gate_down, ple_gate_up,
              rw_mix, rw_w_rkv, rw_w_o, rw_w0, rw_w1, rw_w2, rw_a0, rw_a1, rw_a2, rw_g1, rw_g2,
              rw_k_k, rw_k_a, rw_r_k, rw_lnx_g, rw_lnx_b, rw_v0, rw_v1, rw_v2,
              nsa_w_kv, cmp_pos, cmp_w1, cmp_b1, cmp_w2, nsa_w_qg, nsa_b_g, nsa_w_o):
    h = x
    v_first = None
    shared = None
    for i in range(DEPTH):
        h = layer_norm(ALPHA * h + 0.5 * swiglu(h, ffn1_w13[i], ffn1_w2[i]), ln_g[i, 0], ln_b[i, 0])
        if i < N_A_LAYERS:
            v_lora = None if i == 0 else (rw_v0[i - 1], rw_v1[i - 1], rw_v2[i - 1])
            mix_out, v_first = rwkv7_time_mix(h, v_first, rw_mix[i], rw_w_rkv[i], rw_w_o[i], rw_w0[i], rw_w1[i], rw_w2[i],
                                              rw_a0[i], rw_a1[i], rw_a2[i], rw_g1[i], rw_g2[i], rw_k_k[i], rw_k_a[i],
                                              rw_r_k[i], rw_lnx_g[i], rw_lnx_b[i], v_lora)
        else:
            j = i - N_A_LAYERS
            k_cmp, v_cmp, k_slc, v_slc, k_win, v_win = shared
            mix_out = nsa_attention(h, k_cmp, v_cmp, k_slc, v_slc, k_win, v_win, nsa_w_qg[j], nsa_b_g[j], nsa_w_o[j])
        h = layer_norm(ALPHA * h + mix_out, ln_g[i, 1], ln_b[i, 1])
        h = layer_norm(ALPHA * h + 0.5 * swiglu(h, ffn2_w13[i], ffn2_w2[i]), ln_g[i, 2], ln_b[i, 2])
        ple = jax.nn.sigmoid((h @ ple_gate_down[i]) @ ple_gate_up[i]) * (p[i] @ ple_w[i])
        h = layer_norm(ALPHA * h + ple, ln_g[i, 3], ln_b[i, 3])
        if i == N_A_LAYERS - 1:
            shared = nsa_shared_kv(h, nsa_w_kv, cmp_pos, cmp_w1, cmp_b1, cmp_w2)
    return h
```

```python
import functools
import math

import jax
import jax.numpy as jnp
from jax import lax
from jax.experimental import pallas as pl
from jax.experimental.pallas import tpu as pltpu

F32 = jnp.float32
BF16 = jnp.bfloat16

V7X_VMEM_BYTES = 64 * 1024 * 1024
VMEM_LIMIT = V7X_VMEM_BYTES - 8 * 1024 * 1024
LANES = 128

LN_EPS = 1e-5
DEPTH_TOTAL = 4
ALPHA = (2.0 * DEPTH_TOTAL) ** 0.25

RW_HEAD = 64
RW_GN_EPS = 64e-5
RW_CHUNK = 64

NSA_DK = 128
NSA_HPG = 8
CMP_BLOCK = 32
CMP_STRIDE = 16
SEL_BLOCK = 64
SEL_TOP_N = 16
WINDOW = 512
Q_BLOCK = 128
FORCE_SCORE = 1e4
NEG_INF = -1e30
SEL_TILE = 512

HI = lax.Precision.HIGHEST


def _cparams(sem):
    return pltpu.CompilerParams(dimension_semantics=sem, vmem_limit_bytes=VMEM_LIMIT)


def _pick(n, pref):
    if n <= pref:
        return n
    t = pref
    while n % t:
        t //= 2
    return t


def _mm_kernel(*refs, nk, act, has_abias, has_bias):
    a_ref, b_ref = refs[0], refs[1]
    i = 2
    abias_ref = bias_ref = None
    if has_abias:
        abias_ref = refs[i]; i += 1
    if has_bias:
        bias_ref = refs[i]; i += 1
    o_ref, acc_ref = refs[i], refs[i + 1]
    k = pl.program_id(2)
    a = a_ref[...]
    if has_abias:
        a = a.astype(F32) + abias_ref[...]
    part = jnp.dot(a.astype(BF16), b_ref[...], preferred_element_type=F32)

    @pl.when(k == 0)
    def _():
        acc_ref[...] = part

    @pl.when(k > 0)
    def _():
        acc_ref[...] += part

    @pl.when(k == nk - 1)
    def _():
        z = acc_ref[...]
        if has_bias:
            z = z + bias_ref[...]
        if act is not None:
            z = act(z)
        o_ref[...] = z.astype(o_ref.dtype)


def matmul(a, b, *, out_dtype=F32, act=None, a_bias=None, bias=None, tm=1024, tn=1024, tk=512):
    m, kd = a.shape
    _, n = b.shape
    tm, tn, tk = _pick(m, tm), _pick(n, tn), _pick(kd, tk)
    nk = kd // tk
    in_specs = [pl.BlockSpec((tm, tk), lambda i, j, k: (i, k)),
                pl.BlockSpec((tk, tn), lambda i, j, k: (k, j))]
    args = [a, b]
    if a_bias is not None:
        in_specs.append(pl.BlockSpec((1, tk), lambda i, j, k: (0, k)))
        args.append(a_bias.reshape(1, kd).astype(F32))
    if bias is not None:
        in_specs.append(pl.BlockSpec((1, tn), lambda i, j, k: (0, j)))
        args.append(bias.reshape(1, n).astype(F32))
    return pl.pallas_call(
        functools.partial(_mm_kernel, nk=nk, act=act, has_abias=a_bias is not None, has_bias=bias is not None),
        grid=(m // tm, n // tn, nk),
        in_specs=in_specs,
        out_specs=pl.BlockSpec((tm, tn), lambda i, j, k: (i, j)),
        out_shape=jax.ShapeDtypeStruct((m, n), out_dtype),
        scratch_shapes=[pltpu.VMEM((tm, tn), F32)],
        compiler_params=_cparams(("parallel", "parallel", "arbitrary")),
        name="matmul",
    )(*args)


def _swiglu_kernel(a_ref, b1_ref, b3_ref, o_ref, acc1_ref, acc3_ref, *, nk):
    k = pl.program_id(2)
    a = a_ref[...].astype(BF16)
    p1 = jnp.dot(a, b1_ref[...], preferred_element_type=F32)
    p3 = jnp.dot(a, b3_ref[...], preferred_element_type=F32)

    @pl.when(k == 0)
    def _():
        acc1_ref[...] = p1
        acc3_ref[...] = p3

    @pl.when(k > 0)
    def _():
        acc1_ref[...] += p1
        acc3_ref[...] += p3

    @pl.when(k == nk - 1)
    def _():
        g = acc1_ref[...]
        o_ref[...] = (g * jax.nn.sigmoid(g) * acc3_ref[...]).astype(o_ref.dtype)


def swiglu_up(a, w13, *, tm=1024, tn=1024, tk=512):
    m, kd = a.shape
    f = w13.shape[1] // 2
    tm, tn, tk = _pick(m, tm), _pick(f, tn), _pick(kd, tk)
    nk, nj = kd // tk, f // tn
    return pl.pallas_call(
        functools.partial(_swiglu_kernel, nk=nk),
        grid=(m // tm, nj, nk),
        in_specs=[pl.BlockSpec((tm, tk), lambda i, j, k: (i, k)),
                  pl.BlockSpec((tk, tn), lambda i, j, k: (k, j)),
                  pl.BlockSpec((tk, tn), lambda i, j, k: (k, j + nj))],
        out_specs=pl.BlockSpec((tm, tn), lambda i, j, k: (i, j)),
        out_shape=jax.ShapeDtypeStruct((m, f), BF16),
        scratch_shapes=[pltpu.VMEM((tm, tn), F32), pltpu.VMEM((tm, tn), F32)],
        compiler_params=_cparams(("parallel", "parallel", "arbitrary")),
        name="swiglu_up",
    )(a, w13, w13)


def _deepnorm(z, g, b):
    mu = jnp.mean(z, axis=-1, keepdims=True)
    zc = z - mu
    var = jnp.mean(zc * zc, axis=-1, keepdims=True)
    return zc * lax.rsqrt(var + LN_EPS) * g + b


def _mm_res_ln_kernel(a_ref, b_ref, h_ref, g_ref, beta_ref, o_ref, ob_ref, acc_ref, *, nk, scale):
    k = pl.program_id(1)
    part = jnp.dot(a_ref[...].astype(BF16), b_ref[...], preferred_element_type=F32)

    @pl.when(k == 0)
    def _():
        acc_ref[...] = part

    @pl.when(k > 0)
    def _():
        acc_ref[...] += part

    @pl.when(k == nk - 1)
    def _():
        z = ALPHA * h_ref[...] + scale * acc_ref[...]
        y = _deepnorm(z, g_ref[...], beta_ref[...])
        o_ref[...] = y
        ob_ref[...] = y.astype(BF16)


def mm_res_ln(a, b, h, g, beta, *, scale, tm=256, tk=512):
    m, kd = a.shape
    n = b.shape[1]
    tm, tk = _pick(m, tm), _pick(kd, tk)
    nk = kd // tk
    return pl.pallas_call(
        functools.partial(_mm_res_ln_kernel, nk=nk, scale=scale),
        grid=(m // tm, nk),
        in_specs=[pl.BlockSpec((tm, tk), lambda i, k: (i, k)),
                  pl.BlockSpec((tk, n), lambda i, k: (k, 0)),
                  pl.BlockSpec((tm, n), lambda i, k: (i, 0)),
                  pl.BlockSpec((1, n), lambda i, k: (0, 0)),
                  pl.BlockSpec((1, n), lambda i, k: (0, 0))],
        out_specs=[pl.BlockSpec((tm, n), lambda i, k: (i, 0)),
                   pl.BlockSpec((tm, n), lambda i, k: (i, 0))],
        out_shape=[jax.ShapeDtypeStruct((m, n), F32), jax.ShapeDtypeStruct((m, n), BF16)],
        scratch_shapes=[pltpu.VMEM((tm, n), F32)],
        compiler_params=_cparams(("parallel", "arbitrary")),
        name="mm_res_ln",
    )(a, b, h, g.reshape(1, n), beta.reshape(1, n))


def _ple_kernel(t_ref, p_ref, gu_ref, pw_ref, h_ref, g_ref, beta_ref, o_ref, ob_ref):
    gate = jax.nn.sigmoid(jnp.dot(t_ref[...], gu_ref[...], preferred_element_type=F32))
    emb = jnp.dot(p_ref[...].astype(BF16), pw_ref[...], preferred_element_type=F32)
    y = _deepnorm(ALPHA * h_ref[...] + gate * emb, g_ref[...], beta_ref[...])
    o_ref[...] = y
    ob_ref[...] = y.astype(BF16)


def ple_ln(t, p, gu, pw, h, g, beta, *, tm=256):
    m, n = h.shape
    e = t.shape[1]
    tm = _pick(m, tm)
    row = lambda i: (i, 0)
    fixed = lambda i: (0, 0)
    return pl.pallas_call(
        _ple_kernel,
        grid=(m // tm,),
        in_specs=[pl.BlockSpec((tm, e), row), pl.BlockSpec((tm, e), row),
                  pl.BlockSpec((e, n), fixed), pl.BlockSpec((e, n), fixed),
                  pl.BlockSpec((tm, n), row), pl.BlockSpec((1, n), fixed), pl.BlockSpec((1, n), fixed)],
        out_specs=[pl.BlockSpec((tm, n), row), pl.BlockSpec((tm, n), row)],
        out_shape=[jax.ShapeDtypeStruct((m, n), F32), jax.ShapeDtypeStruct((m, n), BF16)],
        compiler_params=_cparams(("parallel",)),
        name="ple_ln",
    )(t, p, gu, pw, h, g.reshape(1, n), beta.reshape(1, n))


def _rw_mix_kernel(h_ref, hs_ref, mix_ref, o_ref):
    x = h_ref[...]
    xx = hs_ref[...] - x
    for c in range(6):
        o_ref[c] = (x + xx * mix_ref[c:c + 1, :]).astype(BF16)


def rw_mix(h, h_shift, mix, *, tm=256, tn=1024):
    m, n = h.shape
    tm, tn = _pick(m, tm), _pick(n, tn)
    return pl.pallas_call(
        _rw_mix_kernel,
        grid=(m // tm, n // tn),
        in_specs=[pl.BlockSpec((tm, tn), lambda i, j: (i, j)),
                  pl.BlockSpec((tm, tn), lambda i, j: (i, j)),
                  pl.BlockSpec((8, tn), lambda i, j: (0, j))],
        out_specs=pl.BlockSpec((6, tm, tn), lambda i, j: (0, i, j)),
        out_shape=jax.ShapeDtypeStruct((6, m, n), BF16),
        compiler_params=_cparams(("parallel", "parallel")),
        name="rw_mix",
    )(h, h_shift, jnp.pad(mix, ((0, 2), (0, 0))))


def _dot_hi(a, b, dims=(((1,), (0,)), ((), ()))):
    return lax.dot_general(a, b, dims, precision=HI, preferred_element_type=F32)


_NT = (((1,), (1,)), ((), ()))
_TN = (((0,), (0,)), ((), ()))

_RP_W0, _RP_A0, _RP_V0, _RP_KK, _RP_KA, _RP_RK, _RP_LNG, _RP_LNB = range(8)


def _rw_chunk_head(r, lw, k, v, kk, a, s0, tri_incl, strict, incl, eye):
    c = r.shape[0]
    cum = _dot_hi(tri_incl, lw)
    p_in = jnp.exp(cum)
    p_ex = jnp.exp(cum - lw)
    p_inv = jnp.exp(-cum)
    a_t = -kk * p_ex
    r_t = r * p_in
    b_h = kk * a * p_inv
    k_h = k * p_inv
    ar = jnp.concatenate([a_t, r_t], axis=0)
    m_b = _dot_hi(ar, b_h, _NT)
    m_k = _dot_hi(ar, k_h, _NT)
    ars = _dot_hi(ar, s0, _NT)
    a_ab = jnp.where(strict, m_b[:c], 0.0)
    a_ak = jnp.where(strict, m_k[:c], 0.0)
    a_rb = jnp.where(incl, m_b[c:], 0.0)
    a_rk = jnp.where(incl, m_k[c:], 0.0)
    rhs = ars[:c] + _dot_hi(a_ak, v)
    x = eye + a_ab
    pw = a_ab
    steps = max(1, int(math.ceil(math.log2(c))) - 1)
    for _ in range(steps):
        pw = _dot_hi(pw, pw)
        x = x + _dot_hi(x, pw)
    u = _dot_hi(x, rhs)
    y = ars[c:] + _dot_hi(a_rb, u) + _dot_hi(a_rk, v)
    uv = jnp.concatenate([u, v], axis=0)
    bk = jnp.concatenate([b_h, k_h], axis=0)
    s1 = (s0 + _dot_hi(uv, bk, _TN)) * p_in[c - 1:c, :]
    return y, s1


def _rw_rec_kernel(*refs, has_vres, n_chunks):
    if has_vres:
        (r_ref, k_ref, v_ref, vf_ref, hw_ref, ha_ref, hg_ref, hv_ref, w2_ref, a2_ref, g2_ref, v2_ref,
         rp_ref, o_ref, s_ref, lw_s, a_s, g_s, v_s, kk_s, k_s) = refs
    else:
        (r_ref, k_ref, v_ref, hw_ref, ha_ref, hg_ref, w2_ref, a2_ref, g2_ref,
         rp_ref, o_ref, s_ref, lw_s, a_s, g_s, v_s, kk_s, k_s) = refs
    n = RW_HEAD
    c = RW_CHUNK
    heads = LANES // n

    @pl.when(pl.program_id(1) == 0)
    def _():
        s_ref[...] = jnp.zeros_like(s_ref)

    rp = rp_ref[...]
    row = lambda i: rp[i:i + 1, :]
    zw = row(_RP_W0) + jnp.dot(hw_ref[...], w2_ref[...], preferred_element_type=F32)
    nz = -zw
    softplus = jnp.maximum(nz, 0.0) + jnp.log1p(jnp.exp(-jnp.abs(nz)))
    lw_s[...] = -jnp.exp(-softplus - 0.5)
    a = jax.nn.sigmoid(row(_RP_A0) + jnp.dot(ha_ref[...], a2_ref[...], preferred_element_type=F32))
    a_s[...] = a
    g_s[...] = jnp.dot(hg_ref[...], g2_ref[...], preferred_element_type=F32)
    k = k_ref[...]
    v = v_ref[...]
    if has_vres:
        mixv = jax.nn.sigmoid(row(_RP_V0) + jnp.dot(hv_ref[...], v2_ref[...], preferred_element_type=F32))
        v = v + (vf_ref[...] - v) * mixv
    v_s[...] = v
    kk_s[...] = k * row(_RP_KK)
    k_s[...] = k * (1.0 + (a - 1.0) * row(_RP_KA))

    ti = lax.broadcasted_iota(jnp.int32, (c, c), 0)
    si = lax.broadcasted_iota(jnp.int32, (c, c), 1)
    strict = si < ti
    incl = si <= ti
    tri_incl = incl.astype(F32)
    eye = (si == ti).astype(F32)

    def chunk(ci, carry):
        rows = pl.ds(pl.multiple_of(ci * c, c), c)
        for j in range(heads):
            ln = slice(j * n, (j + 1) * n)
            rj, lwj, aj, gj, vj, kkj, kj = (z[rows, ln] for z in (r_ref, lw_s, a_s, g_s, v_s, kk_s, k_s))
            kkj = kkj * lax.rsqrt(jnp.maximum(jnp.sum(kkj * kkj, axis=-1, keepdims=True), 1e-24))
            y, s1 = _rw_chunk_head(rj, lwj, kj, vj, kkj, aj, s_ref[j], tri_incl, strict, incl, eye)
            s_ref[j] = s1
            mu = jnp.mean(y, axis=-1, keepdims=True)
            yc = y - mu
            var = jnp.mean(yc * yc, axis=-1, keepdims=True)
            yn = yc * lax.rsqrt(var + RW_GN_EPS) * row(_RP_LNG)[:, ln] + row(_RP_LNB)[:, ln]
            bonus = jnp.sum(rj * kj * row(_RP_RK)[:, ln], axis=-1, keepdims=True) * vj
            o_ref[rows, ln] = ((yn + bonus) * gj).astype(o_ref.dtype)
        return carry

    lax.fori_loop(0, n_chunks, chunk, 0)


def rw_recurrence(r, k, v, v_first, hw, ha, hg, hv, w2, a2, g2, v2, rowp, *, tb=256):
    t, d = r.shape
    tb = _pick(t, tb)
    has_vres = v_first is not None
    tile = pl.BlockSpec((tb, LANES), lambda hb, i: (i, hb))
    lora = lambda w: pl.BlockSpec((tb, w), lambda hb, i: (i, 0))
    wcol = lambda w: pl.BlockSpec((w, LANES), lambda hb, i: (0, hb))
    lw_, la_, lg_ = hw.shape[1], ha.shape[1], hg.shape[1]
    if has_vres:
        args = [r, k, v, v_first, hw, ha, hg, hv, w2, a2, g2, v2, rowp]
        in_specs = [tile, tile, tile, tile, lora(lw_), lora(la_), lora(lg_), lora(hv.shape[1]),
                    wcol(lw_), wcol(la_), wcol(lg_), wcol(hv.shape[1]), wcol(8)]
    else:
        args = [r, k, v, hw, ha, hg, w2, a2, g2, rowp]
        in_specs = [tile, tile, tile, lora(lw_), lora(la_), lora(lg_), wcol(lw_), wcol(la_), wcol(lg_), wcol(8)]
    return pl.pallas_call(
        functools.partial(_rw_rec_kernel, has_vres=has_vres, n_chunks=tb // RW_CHUNK),
        grid=(d // LANES, t // tb),
        in_specs=in_specs,
        out_specs=tile,
        out_shape=jax.ShapeDtypeStruct((t, d), BF16),
        scratch_shapes=[pltpu.VMEM((LANES // RW_HEAD, RW_HEAD, RW_HEAD), F32)]
        + [pltpu.VMEM((tb, LANES), F32)] * 6,
        compiler_params=_cparams(("parallel", "arbitrary")),
        name="rw_recurrence",
    )(*args)


def _pad_cols(w, mult=LANES):
    p = (-w.shape[-1]) % mult
    return jnp.pad(w, ((0, 0), (0, p))) if p else w


def _pad_rows(w, mult=LANES):
    p = (-w.shape[0]) % mult
    return jnp.pad(w, ((0, p), (0, 0))) if p else w


def rwkv7_time_mix(h, v_first, mix, w_rkv, w_o, w0, w1, w2, a0, a1, a2, g1, g2, k_k, k_a, r_k, lnx_g, lnx_b, v_lora):
    t, d = h.shape
    h_shift = jnp.concatenate([jnp.zeros((1, d), h.dtype), h[:-1]], axis=0)
    xm = rw_mix(h, h_shift, mix)
    wb = w_rkv.astype(BF16)
    r = matmul(xm[0], wb[0])
    k = matmul(xm[1], wb[1])
    v = matmul(xm[2], wb[2])
    hw = matmul(xm[3], _pad_cols(w1).astype(BF16), out_dtype=BF16, act=jnp.tanh)
    ha = matmul(xm[4], _pad_cols(a1).astype(BF16), out_dtype=BF16)
    hg = matmul(xm[5], _pad_cols(g1).astype(BF16), out_dtype=BF16, act=jax.nn.sigmoid)
    zero = jnp.zeros((d,), F32)
    rowp = jnp.stack([w0, a0, zero if v_lora is None else v_lora[0], k_k, k_a, r_k.reshape(d), lnx_g, lnx_b])
    if v_lora is None:
        out = rw_recurrence(r, k, v, None, hw, ha, hg, None, _pad_rows(w2).astype(BF16),
                            _pad_rows(a2).astype(BF16), _pad_rows(g2).astype(BF16), None, rowp)
        v_first = v
    else:
        hv = matmul(xm[2], _pad_cols(v_lora[1]).astype(BF16), out_dtype=BF16)
        out = rw_recurrence(r, k, v, v_first, hw, ha, hg, hv, _pad_rows(w2).astype(BF16),
                            _pad_rows(a2).astype(BF16), _pad_rows(g2).astype(BF16),
                            _pad_rows(v_lora[2]).astype(BF16), rowp)
    return out, v_first


def _softmax_rows(s, valid):
    s = jnp.where(valid, s, NEG_INF)
    m = jnp.max(s, axis=-1, keepdims=True)
    p = jnp.where(valid, jnp.exp(s - m), 0.0)
    l = jnp.sum(p, axis=-1, keepdims=True)
    return p / jnp.maximum(l, 1e-30)


def _nsa_kernel(q_ref, gate_ref, kc_ref, vc_ref, ks_ref, vs_ref, *rest, n_heads_total):
    kw_refs = rest[0:5]
    vw_refs = rest[5:10]
    o_ref, m_s, l_s, acc_s = rest[10:14]
    g = pl.program_id(0)
    qb = pl.program_id(1)
    hpg, dk, qn = NSA_HPG, NSA_DK, Q_BLOCK
    rows = hpg * qn
    scale = dk ** -0.5
    t0 = qb * qn

    q2 = q_ref[...]
    qs = jnp.concatenate([q2[:, h * dk:(h + 1) * dk] for h in range(hpg)], axis=0)
    ri = lax.broadcasted_iota(jnp.int32, (rows, 1), 0)
    head = g * hpg + ri // qn
    slope = jnp.exp2(-8.0 * (head + 1).astype(F32) / n_heads_total)
    trow = t0 + ri % qn

    kc = kc_ref[0]
    n_cp = kc.shape[0]
    s = lax.dot_general(qs, kc, _NT, preferred_element_type=F32) * scale
    pos_c = lax.broadcasted_iota(jnp.int32, (1, n_cp), 1) * CMP_STRIDE + (CMP_BLOCK - 1)
    d_c = trow - pos_c
    p_c = _softmax_rows(s - slope * d_c.astype(F32), d_c >= 0)
    o_c = jnp.dot(p_c.astype(BF16), vc_ref[0], preferred_element_type=F32)

    p_grp = p_c[0:qn]
    for h in range(1, hpg):
        p_grp = p_grp + p_c[h * qn:(h + 1) * qn]
    n_sel = (n_cp * CMP_STRIDE) // SEL_BLOCK
    ratio = SEL_BLOCK // CMP_STRIDE
    n_ov = CMP_BLOCK // CMP_STRIDE
    ci = lax.broadcasted_iota(jnp.int32, (n_cp, n_sel), 0)
    ji = lax.broadcasted_iota(jnp.int32, (n_cp, n_sel), 1)
    off = ci - ratio * ji
    cnt = jnp.maximum(jnp.minimum(jnp.minimum(off + n_ov, ratio - off), jnp.minimum(n_ov, ratio)), 0)
    imp = _dot_hi(p_grp, cnt.astype(F32))
    tq = t0 + lax.broadcasted_iota(jnp.int32, (qn, 1), 0)
    cur = tq // SEL_BLOCK
    sj = lax.broadcasted_iota(jnp.int32, (qn, n_sel), 1)
    forced = (sj == 0) | (sj == cur) | (sj == cur - 1)
    work = jnp.where(sj > cur, -1.0, jnp.where(forced, FORCE_SCORE, imp))
    sel = jnp.zeros((qn, n_sel), F32)
    sjf = sj.astype(F32)
    for _ in range(min(SEL_TOP_N, n_sel)):
        mx = jnp.max(work, axis=-1, keepdims=True)
        first = jnp.min(jnp.where(work == mx, sjf, float(n_sel)), axis=-1, keepdims=True)
        pick = sjf == first
        sel = jnp.where(pick, 1.0, sel)
        work = jnp.where(pick, -2.0, work)
    sel_b = sel.astype(BF16)

    m_s[...] = jnp.full_like(m_s, NEG_INF)
    l_s[...] = jnp.zeros_like(l_s)
    acc_s[...] = jnp.zeros_like(acc_s)
    n_keys = ks_ref.shape[0]
    tk = min(SEL_TILE, n_keys)
    bpt = tk // SEL_BLOCK

    def sweep(i, carry):
        k0 = pl.multiple_of(i * tk, tk)
        kt = ks_ref[pl.ds(k0, tk), :]
        vt = vs_ref[pl.ds(k0, tk), :]
        st = lax.dot_general(qs, kt, _NT, preferred_element_type=F32) * scale
        bi = lax.broadcasted_iota(jnp.int32, (n_sel, tk), 0)
        ki = lax.broadcasted_iota(jnp.int32, (n_sel, tk), 1)
        expand = (bi == i * bpt + ki // SEL_BLOCK).astype(BF16)
        chosen = jnp.dot(sel_b, expand, preferred_element_type=F32)
        chosen = jnp.concatenate([chosen] * hpg, axis=0)
        d_s = trow - (k0 + lax.broadcasted_iota(jnp.int32, (1, tk), 1))
        valid = (d_s >= 0) & (chosen > 0.5)
        st = jnp.where(valid, st - slope * d_s.astype(F32), NEG_INF)
        m_old = m_s[...]
        m_new = jnp.maximum(m_old, jnp.max(st, axis=-1, keepdims=True))
        alpha = jnp.exp(m_old - m_new)
        p = jnp.where(valid, jnp.exp(st - m_new), 0.0)
        l_s[...] = alpha * l_s[...] + jnp.sum(p, axis=-1, keepdims=True)
        acc_s[...] = alpha * acc_s[...] + jnp.dot(p.astype(BF16), vt, preferred_element_type=F32)
        m_s[...] = m_new
        return carry

    lax.fori_loop(0, (t0 + qn + tk - 1) // tk, sweep, 0)
    o_s = acc_s[...] / jnp.maximum(l_s[...], 1e-30)

    kw = jnp.concatenate([r[...] for r in kw_refs], axis=0)
    vw = jnp.concatenate([r[...] for r in vw_refs], axis=0)
    sw = lax.dot_general(qs, kw, _NT, preferred_element_type=F32) * scale
    pos_w = t0 - WINDOW + lax.broadcasted_iota(jnp.int32, (1, WINDOW + qn), 1)
    d_w = trow - pos_w
    valid_w = (d_w >= 0) & (d_w < WINDOW) & (pos_w >= 0)
    p_w = _softmax_rows(sw - slope * d_w.astype(F32), valid_w)
    o_w = jnp.dot(p_w.astype(BF16), vw, preferred_element_type=F32)

    gate = gate_ref[...]
    for h in range(hpg):
        sl = slice(h * qn, (h + 1) * qn)
        g_c, g_s, g_w = (gate[:, b * hpg + h:b * hpg + h + 1] for b in range(3))
        o = g_c * o_c[sl] + g_s * o_s[sl] + g_w * o_w[sl]
        o_ref[:, h * dk:(h + 1) * dk] = o.astype(o_ref.dtype)


def nsa_attention_core(q, gate, k_cmp, v_cmp, kvb, n_groups):
    t, d = q.shape
    gq = NSA_HPG * NSA_DK
    n_qb = t // Q_BLOCK
    nwb = WINDOW // Q_BLOCK
    n_cp = k_cmp.shape[1]
    full = lambda j: pl.BlockSpec((t, NSA_DK), lambda g, qb, j=j: (0, j * n_groups + g))

    def win(j, i):
        return pl.BlockSpec((Q_BLOCK, NSA_DK), lambda g, qb, j=j, i=i: (jnp.maximum(qb - nwb + i, 0), j * n_groups + g))

    in_specs = [pl.BlockSpec((Q_BLOCK, gq), lambda g, qb: (qb, g)),
                pl.BlockSpec((Q_BLOCK, LANES), lambda g, qb: (qb, g)),
                pl.BlockSpec((1, n_cp, NSA_DK), lambda g, qb: (g, 0, 0)),
                pl.BlockSpec((1, n_cp, NSA_DK), lambda g, qb: (g, 0, 0)),
                full(2), full(3)]
    in_specs += [win(4, i) for i in range(nwb + 1)] + [win(5, i) for i in range(nwb + 1)]
    rows = NSA_HPG * Q_BLOCK
    return pl.pallas_call(
        functools.partial(_nsa_kernel, n_heads_total=n_groups * NSA_HPG),
        grid=(n_groups, n_qb),
        in_specs=in_specs,
        out_specs=pl.BlockSpec((Q_BLOCK, gq), lambda g, qb: (qb, g)),
        out_shape=jax.ShapeDtypeStruct((t, d), BF16),
        scratch_shapes=[pltpu.VMEM((rows, 1), F32), pltpu.VMEM((rows, 1), F32), pltpu.VMEM((rows, NSA_DK), F32)],
        compiler_params=_cparams(("parallel", "arbitrary")),
        name="nsa_attention",
    )(q, gate, k_cmp, v_cmp, kvb, kvb, *([kvb] * (2 * (nwb + 1))))


def nsa_shared_kv(hb, w_kv, cmp_pos, cmp_w1, cmp_b1, cmp_w2):
    t = hb.shape[0]
    n_groups = w_kv.shape[1] // (6 * NSA_DK)
    kv = matmul(hb, w_kv.astype(BF16))
    n_chunk = t // CMP_STRIDE
    cmp = []
    for j in range(2):
        z = kv[:, j * n_groups * NSA_DK:(j + 1) * n_groups * NSA_DK]
        ch = z.reshape(n_chunk, CMP_STRIDE, n_groups, NSA_DK).transpose(2, 0, 1, 3).reshape(n_groups, n_chunk, CMP_STRIDE * NSA_DK)
        nxt = jnp.concatenate([ch[:, 1:], jnp.zeros_like(ch[:, :1])], axis=1)
        flat = jnp.concatenate([ch, nxt], axis=-1).reshape(n_groups * n_chunk, CMP_BLOCK * NSA_DK)
        hid = matmul(flat, cmp_w1[j].astype(BF16), out_dtype=BF16, act=jax.nn.silu,
                     a_bias=cmp_pos[j].reshape(-1), bias=cmp_b1[j])
        out = matmul(hid, cmp_w2[j].astype(BF16))
        cmp.append(out.reshape(n_groups, n_chunk, NSA_DK).astype(BF16))
    return cmp[0], cmp[1], kv.astype(BF16)


def nsa_layer(hb, shared, w_qg, b_g):
    k_cmp, v_cmp, kvb = shared
    n_groups = k_cmp.shape[0]
    n_heads = n_groups * NSA_HPG
    dq = n_heads * NSA_DK
    q = matmul(hb, w_qg[:, :dq].astype(BF16), out_dtype=BF16)
    pad = LANES - 3 * NSA_HPG
    w_g = w_qg[:, dq:].reshape(-1, n_groups, NSA_HPG, 3).transpose(0, 1, 3, 2).reshape(-1, n_groups, 3 * NSA_HPG)
    w_g = jnp.pad(w_g, ((0, 0), (0, 0), (0, pad))).reshape(-1, n_groups * LANES)
    b_gp = b_g.reshape(n_groups, NSA_HPG, 3).transpose(0, 2, 1).reshape(n_groups, 3 * NSA_HPG)
    b_gp = jnp.pad(b_gp, ((0, 0), (0, pad))).reshape(-1)
    gate = matmul(hb, w_g.astype(BF16), act=jax.nn.sigmoid, bias=b_gp)
    return nsa_attention_core(q, gate, k_cmp, v_cmp, kvb, n_groups)


def kernel(x, p, ln_g, ln_b, ffn1_w13, ffn1_w2, ffn2_w13, ffn2_w2, ple_w, ple_gate_down, ple_gate_up, rw_mix, rw_w_rkv, rw_w_o, rw_w0, rw_w1, rw_w2, rw_a0, rw_a1, rw_a2, rw_g1, rw_g2, rw_k_k, rw_k_a, rw_r_k, rw_lnx_g, rw_lnx_b, rw_v0, rw_v1, rw_v2, nsa_w_kv, cmp_pos, cmp_w1, cmp_b1, cmp_w2, nsa_w_qg, nsa_b_g, nsa_w_o):
    batch, t, d = x.shape
    depth = ln_g.shape[0]
    n_a = rw_mix.shape[0]
    outs = []
    for bi in range(batch):
        h = x[bi]
        hb = h.astype(BF16)
        v_first = None
        shared = None
        for i in range(depth):
            mid = swiglu_up(hb, ffn1_w13[i].astype(BF16))
            h, hb = mm_res_ln(mid, ffn1_w2[i].astype(BF16), h, ln_g[i, 0], ln_b[i, 0], scale=0.5)
            if i < n_a:
                v_lora = None if i == 0 else (rw_v0[i - 1], rw_v1[i - 1], rw_v2[i - 1])
                mix_pre, v_first = rwkv7_time_mix(
                    h, v_first, rw_mix[i], rw_w_rkv[i], rw_w_o[i], rw_w0[i], rw_w1[i], rw_w2[i], rw_a0[i],
                    rw_a1[i], rw_a2[i], rw_g1[i], rw_g2[i], rw_k_k[i], rw_k_a[i], rw_r_k[i], rw_lnx_g[i],
                    rw_lnx_b[i], v_lora)
                w_o = rw_w_o[i]
            else:
                j = i - n_a
                mix_pre = nsa_layer(hb, shared, nsa_w_qg[j], nsa_b_g[j])
                w_o = nsa_w_o[j]
            h, hb = mm_res_ln(mix_pre, w_o.astype(BF16), h, ln_g[i, 1], ln_b[i, 1], scale=1.0)
            mid = swiglu_up(hb, ffn2_w13[i].astype(BF16))
            h, hb = mm_res_ln(mid, ffn2_w2[i].astype(BF16), h, ln_g[i, 2], ln_b[i, 2], scale=0.5)
            tdown = matmul(hb, ple_gate_down[i].astype(BF16), out_dtype=BF16)
            h, hb = ple_ln(tdown, p[i, bi], ple_gate_up[i].astype(BF16), ple_w[i].astype(BF16), h,
                           ln_g[i, 3], ln_b[i, 3])
            if i == n_a - 1:
                shared = nsa_shared_kv(hb, nsa_w_kv, cmp_pos, cmp_w1, cmp_b1, cmp_w2)
        outs.append(h)
    return jnp.stack(outs, axis=0)
```

```python
import functools
import math

import jax
import jax.numpy as jnp
from jax import lax
from jax.experimental import pallas as pl
from jax.experimental.pallas import tpu as pltpu

F32 = jnp.float32
BF16 = jnp.bfloat16

V7X_VMEM_BYTES = 64 * 1024 * 1024
VMEM_LIMIT = V7X_VMEM_BYTES - 8 * 1024 * 1024
LANES = 128

LN_EPS = 1e-5
DEPTH_TOTAL = 4
ALPHA = (2.0 * DEPTH_TOTAL) ** 0.25

RW_HEAD = 64
RW_GN_EPS = 64e-5
RW_CHUNK = 64

NSA_DK = 128
NSA_HPG = 8
CMP_BLOCK = 32
CMP_STRIDE = 16
SEL_BLOCK = 64
SEL_TOP_N = 16
WINDOW = 512
Q_BLOCK = 128
FORCE_SCORE = 1e4
NEG_INF = -1e30
SEL_TILE = 512

_NN = (((1,), (0,)), ((), ()))
_NT = (((1,), (1,)), ((), ()))
_TN = (((0,), (0,)), ((), ()))


def _cparams(sem):
    return pltpu.CompilerParams(dimension_semantics=sem, vmem_limit_bytes=VMEM_LIMIT)


def _pick(n, pref):
    if n <= pref:
        return n
    t = pref
    while n % t:
        t //= 2
    return t


def _mm_kernel(*refs, nk, act, has_abias, has_bias):
    a_ref, b_ref = refs[0], refs[1]
    i = 2
    abias_ref = bias_ref = None
    if has_abias:
        abias_ref = refs[i]; i += 1
    if has_bias:
        bias_ref = refs[i]; i += 1
    o_ref, acc_ref = refs[i], refs[i + 1]
    k = pl.program_id(2)
    a = a_ref[...]
    if has_abias:
        a = a.astype(F32) + abias_ref[...]
    part = jnp.dot(a.astype(BF16), b_ref[...], preferred_element_type=F32)

    @pl.when(k == 0)
    def _():
        acc_ref[...] = part

    @pl.when(k > 0)
    def _():
        acc_ref[...] += part

    @pl.when(k == nk - 1)
    def _():
        z = acc_ref[...]
        if has_bias:
            z = z + bias_ref[...]
        if act is not None:
            z = act(z)
        o_ref[...] = z.astype(o_ref.dtype)


def matmul(a, b, *, lead=None, out_dtype=F32, act=None, a_bias=None, bias=None, tm=1024, tn=1024, tk=1024):
    m, kd = a.shape[-2:]
    _, n = b.shape
    tm, tn, tk = _pick(m, tm), _pick(n, tn), _pick(kd, tk)
    nk = kd // tk
    if lead is None:
        a_spec = pl.BlockSpec((tm, tk), lambda i, j, k: (i, k))
    else:
        a_spec = pl.BlockSpec((None, tm, tk), lambda i, j, k: (lead, i, k))
    in_specs = [a_spec, pl.BlockSpec((tk, tn), lambda i, j, k: (k, j))]
    args = [a, b]
    if a_bias is not None:
        in_specs.append(pl.BlockSpec((1, tk), lambda i, j, k: (0, k)))
        args.append(a_bias.reshape(1, kd).astype(F32))
    if bias is not None:
        in_specs.append(pl.BlockSpec((1, tn), lambda i, j, k: (0, j)))
        args.append(bias.reshape(1, n).astype(F32))
    return pl.pallas_call(
        functools.partial(_mm_kernel, nk=nk, act=act, has_abias=a_bias is not None, has_bias=bias is not None),
        grid=(m // tm, n // tn, nk),
        in_specs=in_specs,
        out_specs=pl.BlockSpec((tm, tn), lambda i, j, k: (i, j)),
        out_shape=jax.ShapeDtypeStruct((m, n), out_dtype),
        scratch_shapes=[pltpu.VMEM((tm, tn), F32)],
        compiler_params=_cparams(("parallel", "parallel", "arbitrary")),
        name="matmul",
    )(*args)


def _swiglu_kernel(a_ref, b1_ref, b3_ref, o_ref, acc1_ref, acc3_ref, *, nk):
    k = pl.program_id(2)
    a = a_ref[...].astype(BF16)
    p1 = jnp.dot(a, b1_ref[...], preferred_element_type=F32)
    p3 = jnp.dot(a, b3_ref[...], preferred_element_type=F32)

    @pl.when(k == 0)
    def _():
        acc1_ref[...] = p1
        acc3_ref[...] = p3

    @pl.when(k > 0)
    def _():
        acc1_ref[...] += p1
        acc3_ref[...] += p3

    @pl.when(k == nk - 1)
    def _():
        g = acc1_ref[...]
        o_ref[...] = (g * jax.nn.sigmoid(g) * acc3_ref[...]).astype(o_ref.dtype)


def swiglu_up(a, w13, *, tm=1024, tn=1024, tk=1024):
    m, kd = a.shape
    f = w13.shape[1] // 2
    tm, tn, tk = _pick(m, tm), _pick(f, tn), _pick(kd, tk)
    nk, nj = kd // tk, f // tn
    return pl.pallas_call(
        functools.partial(_swiglu_kernel, nk=nk),
        grid=(m // tm, nj, nk),
        in_specs=[pl.BlockSpec((tm, tk), lambda i, j, k: (i, k)),
                  pl.BlockSpec((tk, tn), lambda i, j, k: (k, j)),
                  pl.BlockSpec((tk, tn), lambda i, j, k: (k, j + nj))],
        out_specs=pl.BlockSpec((tm, tn), lambda i, j, k: (i, j)),
        out_shape=jax.ShapeDtypeStruct((m, f), BF16),
        scratch_shapes=[pltpu.VMEM((tm, tn), F32), pltpu.VMEM((tm, tn), F32)],
        compiler_params=_cparams(("parallel", "parallel", "arbitrary")),
        name="swiglu_up",
    )(a, w13, w13)


def _deepnorm(z, g, b):
    mu = jnp.mean(z, axis=-1, keepdims=True)
    zc = z - mu
    var = jnp.mean(zc * zc, axis=-1, keepdims=True)
    return zc * lax.rsqrt(var + LN_EPS) * g + b


def _mm_res_ln_kernel(a_ref, b_ref, h_ref, g_ref, beta_ref, o_ref, ob_ref, *, nk, scale):
    k = pl.program_id(1)
    part = jnp.dot(a_ref[...].astype(BF16), b_ref[...], preferred_element_type=F32)

    @pl.when(k == 0)
    def _():
        o_ref[...] = part

    @pl.when(k > 0)
    def _():
        o_ref[...] += part

    @pl.when(k == nk - 1)
    def _():
        y = _deepnorm(ALPHA * h_ref[...] + scale * o_ref[...], g_ref[...], beta_ref[...])
        o_ref[...] = y
        ob_ref[...] = y.astype(BF16)


def mm_res_ln(a, b, h, g, beta, *, scale, tm=512, tk=256):
    m, kd = a.shape
    n = b.shape[1]
    tm, tk = _pick(m, tm), _pick(kd, tk)
    nk = kd // tk
    return pl.pallas_call(
        functools.partial(_mm_res_ln_kernel, nk=nk, scale=scale),
        grid=(m // tm, nk),
        in_specs=[pl.BlockSpec((tm, tk), lambda i, k: (i, k)),
                  pl.BlockSpec((tk, n), lambda i, k: (k, 0)),
                  pl.BlockSpec((tm, n), lambda i, k: (i, 0), pipeline_mode=pl.Buffered(1)),
                  pl.BlockSpec((1, n), lambda i, k: (0, 0)),
                  pl.BlockSpec((1, n), lambda i, k: (0, 0))],
        out_specs=[pl.BlockSpec((tm, n), lambda i, k: (i, 0)),
                   pl.BlockSpec((tm, n), lambda i, k: (i, 0))],
        out_shape=[jax.ShapeDtypeStruct((m, n), F32), jax.ShapeDtypeStruct((m, n), BF16)],
        compiler_params=_cparams(("parallel", "arbitrary")),
        name="mm_res_ln",
    )(a, b, h, g.reshape(1, n), beta.reshape(1, n))


def _ple_kernel(t_ref, p_ref, gu_ref, pw_ref, h_ref, g_ref, beta_ref, o_ref, ob_ref):
    gate = jax.nn.sigmoid(jnp.dot(t_ref[...], gu_ref[...], preferred_element_type=F32))
    emb = jnp.dot(p_ref[...].astype(BF16), pw_ref[...], preferred_element_type=F32)
    y = _deepnorm(ALPHA * h_ref[...] + gate * emb, g_ref[...], beta_ref[...])
    o_ref[...] = y
    ob_ref[...] = y.astype(BF16)


def ple_ln(t, p, gu, pw, h, g, beta, *, tm=256):
    m, n = h.shape
    e = t.shape[1]
    tm = _pick(m, tm)
    row = lambda i: (i, 0)
    fixed = lambda i: (0, 0)
    return pl.pallas_call(
        _ple_kernel,
        grid=(m // tm,),
        in_specs=[pl.BlockSpec((tm, e), row), pl.BlockSpec((tm, e), row),
                  pl.BlockSpec((e, n), fixed), pl.BlockSpec((e, n), fixed),
                  pl.BlockSpec((tm, n), row), pl.BlockSpec((1, n), fixed), pl.BlockSpec((1, n), fixed)],
        out_specs=[pl.BlockSpec((tm, n), row), pl.BlockSpec((tm, n), row)],
        out_shape=[jax.ShapeDtypeStruct((m, n), F32), jax.ShapeDtypeStruct((m, n), BF16)],
        compiler_params=_cparams(("parallel",)),
        name="ple_ln",
    )(t, p, gu, pw, h, g.reshape(1, n), beta.reshape(1, n))


def _split2(x):
    hi = x.astype(BF16)
    return hi, (x - hi.astype(F32)).astype(BF16)


def _mm3(a, b, dims=_NN):
    dg = lambda x, y: lax.dot_general(x, y, dims, preferred_element_type=F32)
    return dg(a[0], b[0]) + (dg(a[0], b[1]) + dg(a[1], b[0]))


def _split3(x):
    hi = x.astype(BF16)
    r1 = x - hi.astype(F32)
    mid = r1.astype(BF16)
    return hi, mid, (r1 - mid.astype(F32)).astype(BF16)


def _rw_mix_kernel(h_ref, hs_ref, mix_ref, o_ref):
    x = h_ref[...]
    xx = hs_ref[...] - x
    for c in range(6):
        o_ref[c] = (x + xx * mix_ref[c:c + 1, :]).astype(BF16)


def rw_mix(h, h_shift, mix, *, tm=256, tn=1024):
    m, n = h.shape
    tm, tn = _pick(m, tm), _pick(n, tn)
    return pl.pallas_call(
        _rw_mix_kernel,
        grid=(m // tm, n // tn),
        in_specs=[pl.BlockSpec((tm, tn), lambda i, j: (i, j)),
                  pl.BlockSpec((tm, tn), lambda i, j: (i, j)),
                  pl.BlockSpec((8, tn), lambda i, j: (0, j))],
        out_specs=pl.BlockSpec((6, tm, tn), lambda i, j: (0, i, j)),
        out_shape=jax.ShapeDtypeStruct((6, m, n), BF16),
        compiler_params=_cparams(("parallel", "parallel")),
        name="rw_mix",
    )(h, h_shift, jnp.pad(mix, ((0, 2), (0, 0))))


_RP_W0, _RP_A0, _RP_V0, _RP_KK, _RP_KA, _RP_RK, _RP_LNG, _RP_LNB = range(8)


def _each(f, *lists):
    return [f(*xs) for xs in zip(*lists)]


def _rw_chunk_local(a_t, r_t, b_h, k_h, v, p_last, strict, incl, eye):
    sp = lambda xs: _each(_split2, xs)
    nt = lambda a, b: _mm3(a, b, _NT)
    tn = lambda a, b: _mm3(a, b, _TN)
    at, rt, bh, kh, vs = sp(a_t), sp(r_t), sp(b_h), sp(k_h), sp(v)
    m_ab = _each(lambda z: jnp.where(strict, z, 0.0), _each(nt, at, bh))
    m_ak = _each(lambda z: jnp.where(strict, z, 0.0), _each(nt, at, kh))
    m_rb = _each(lambda z: jnp.where(incl, z, 0.0), _each(nt, rt, bh))
    m_rk = _each(lambda z: jnp.where(incl, z, 0.0), _each(nt, rt, kh))
    x = _each(lambda z: eye + z, m_ab)
    pw = m_ab
    for _ in range(max(1, int(math.ceil(math.log2(a_t[0].shape[0]))) - 1)):
        ps = sp(pw)
        pw = _each(_mm3, ps, ps)
        x = _each(lambda z, d: z + d, x, _each(_mm3, sp(x), sp(pw)))
    xs = sp(x)
    a_x = _each(_mm3, xs, at)
    u0 = _each(_mm3, xs, sp(_each(_mm3, sp(m_ak), vs)))
    axs, u0s, rbs = sp(a_x), sp(u0), sp(m_rb)
    r_hat = _each(lambda z, d: z + d, r_t, _each(_mm3, rbs, axs))
    y0 = _each(lambda z, d: z + d, _each(_mm3, rbs, u0s), _each(_mm3, sp(m_rk), vs))
    t_m = _each(lambda z, p: (eye + z) * p, _each(tn, axs, bh), p_last)
    s_add = _each(lambda z, d, p: (z + d) * p, _each(tn, u0s, bh), _each(tn, vs, kh), p_last)
    return r_hat, y0, t_m, s_add


def _rw_rec_kernel(*refs, has_vres, n_chunks):
    if has_vres:
        (r_ref, k_ref, v_ref, vf_ref, hw_ref, ha_ref, hg_ref, hv_ref, w2_ref, a2_ref, g2_ref, v2_ref,
         rp_ref, o_ref, s_ref, at_s, rt_s, bh_s, kh_s, v_s, g_s, bon_s, rh_s, y0_s, tm_s, sa_s) = refs
    else:
        (r_ref, k_ref, v_ref, hw_ref, ha_ref, hg_ref, w2_ref, a2_ref, g2_ref,
         rp_ref, o_ref, s_ref, at_s, rt_s, bh_s, kh_s, v_s, g_s, bon_s, rh_s, y0_s, tm_s, sa_s) = refs
    n = RW_HEAD
    c = RW_CHUNK
    heads = LANES // n

    @pl.when(pl.program_id(1) == 0)
    def _():
        s_ref[...] = jnp.zeros_like(s_ref)

    rp = rp_ref[...]
    row = lambda i: rp[i:i + 1, :]
    lane = lax.broadcasted_iota(jnp.int32, (1, LANES), 1)

    def per_head_sum(z):
        out = jnp.zeros_like(z)
        for j in range(heads):
            mine = (lane >= j * n) & (lane < (j + 1) * n)
            out = jnp.where(mine, jnp.sum(jnp.where(mine, z, 0.0), axis=-1, keepdims=True), out)
        return out

    zw = row(_RP_W0) + jnp.dot(hw_ref[...], w2_ref[...], preferred_element_type=F32)
    nz = -zw
    softplus = jnp.maximum(nz, 0.0) + jnp.log1p(jnp.exp(-jnp.abs(nz)))
    lw = -jnp.exp(-softplus - 0.5)
    a = jax.nn.sigmoid(row(_RP_A0) + jnp.dot(ha_ref[...], a2_ref[...], preferred_element_type=F32))
    g_s[...] = jnp.dot(hg_ref[...], g2_ref[...], preferred_element_type=F32)
    r = r_ref[...]
    k = k_ref[...]
    v = v_ref[...]
    if has_vres:
        mixv = jax.nn.sigmoid(row(_RP_V0) + jnp.dot(hv_ref[...], v2_ref[...], preferred_element_type=F32))
        v = v + (vf_ref[...] - v) * mixv
    v_s[...] = v
    kk = k * row(_RP_KK)
    kk = kk * lax.rsqrt(jnp.maximum(per_head_sum(kk * kk), 1e-24))
    k = k * (1.0 + (a - 1.0) * row(_RP_KA))
    bon_s[...] = per_head_sum(r * k * row(_RP_RK)) * v

    ti = lax.broadcasted_iota(jnp.int32, (c, c), 0)
    si = lax.broadcasted_iota(jnp.int32, (c, c), 1)
    strict = si < ti
    incl = si <= ti
    tri_incl = incl.astype(BF16)
    eye = (si == ti).astype(F32)

    p_last = []
    for ci in range(n_chunks):
        rows = slice(ci * c, (ci + 1) * c)
        lw_c = lw[rows]
        cum = sum(jnp.dot(tri_incl, part, preferred_element_type=F32) for part in _split3(lw_c))
        p_in = jnp.exp(cum)
        p_inv = jnp.exp(-cum)
        at_s[rows, :] = -kk[rows] * jnp.exp(cum - lw_c)
        rt_s[rows, :] = r[rows] * p_in
        bh_s[rows, :] = kk[rows] * a[rows] * p_inv
        kh_s[rows, :] = k[rows] * p_inv
        p_last.append(p_in[c - 1:c, :])

    where = [(slice(ci * c, (ci + 1) * c), slice(j * n, (j + 1) * n)) for ci in range(n_chunks) for j in range(heads)]
    r_hat, y0, t_m, s_add = _rw_chunk_local(
        [at_s[w] for w in where], [rt_s[w] for w in where], [bh_s[w] for w in where], [kh_s[w] for w in where],
        [v_s[w] for w in where], [p_last[ci][:, j * n:(j + 1) * n] for ci in range(n_chunks) for j in range(heads)],
        strict, incl, eye)
    for i, w in enumerate(where):
        rh_s[w] = r_hat[i]
        y0_s[w] = y0[i]
        tm_s[i] = t_m[i]
        sa_s[i] = s_add[i]

    s = [s_ref[j] for j in range(heads)]
    for ci in range(n_chunks):
        rows = slice(ci * c, (ci + 1) * c)
        for j in range(heads):
            ln = slice(j * n, (j + 1) * n)
            ss = _split2(s[j])
            y = _mm3(_split2(rh_s[rows, ln]), ss, _NT) + y0_s[rows, ln]
            s[j] = _mm3(ss, _split2(tm_s[ci * heads + j])) + sa_s[ci * heads + j]
            mu = jnp.mean(y, axis=-1, keepdims=True)
            yc = y - mu
            var = jnp.mean(yc * yc, axis=-1, keepdims=True)
            yn = yc * lax.rsqrt(var + RW_GN_EPS) * row(_RP_LNG)[:, ln] + row(_RP_LNB)[:, ln]
            o_ref[rows, ln] = ((yn + bon_s[rows, ln]) * g_s[rows, ln]).astype(o_ref.dtype)
    for j in range(heads):
        s_ref[j] = s[j]


def rw_recurrence(r, k, v, v_first, hw, ha, hg, hv, w2, a2, g2, v2, rowp, *, tb=256):
    t, d = r.shape
    tb = _pick(t, tb)
    has_vres = v_first is not None
    tile = pl.BlockSpec((tb, LANES), lambda hb, i: (i, hb))
    lora = lambda w: pl.BlockSpec((tb, w), lambda hb, i: (i, 0))
    wcol = lambda w: pl.BlockSpec((w, LANES), lambda hb, i: (0, hb))
    lw_, la_, lg_ = hw.shape[1], ha.shape[1], hg.shape[1]
    if has_vres:
        args = [r, k, v, v_first, hw, ha, hg, hv, w2, a2, g2, v2, rowp]
        in_specs = [tile, tile, tile, tile, lora(lw_), lora(la_), lora(lg_), lora(hv.shape[1]),
                    wcol(lw_), wcol(la_), wcol(lg_), wcol(hv.shape[1]), wcol(8)]
    else:
        args = [r, k, v, hw, ha, hg, w2, a2, g2, rowp]
        in_specs = [tile, tile, tile, lora(lw_), lora(la_), lora(lg_), wcol(lw_), wcol(la_), wcol(lg_), wcol(8)]
    heads = LANES // RW_HEAD
    n_chunks = tb // RW_CHUNK
    per_chunk = pltpu.VMEM((n_chunks * heads, RW_HEAD, RW_HEAD), F32)
    return pl.pallas_call(
        functools.partial(_rw_rec_kernel, has_vres=has_vres, n_chunks=n_chunks),
        grid=(d // LANES, t // tb),
        in_specs=in_specs,
        out_specs=tile,
        out_shape=jax.ShapeDtypeStruct((t, d), BF16),
        scratch_shapes=[pltpu.VMEM((heads, RW_HEAD, RW_HEAD), F32)]
        + [pltpu.VMEM((tb, LANES), F32)] * 9 + [per_chunk, per_chunk],
        compiler_params=_cparams(("parallel", "arbitrary")),
        name="rw_recurrence",
    )(*args)


def _pad_cols(w, mult=LANES):
    p = (-w.shape[-1]) % mult
    return jnp.pad(w, ((0, 0), (0, p))) if p else w


def _pad_rows(w, mult=LANES):
    p = (-w.shape[0]) % mult
    return jnp.pad(w, ((0, p), (0, 0))) if p else w


def rwkv7_time_mix(h, v_first, mix, w_rkv, w_o, w0, w1, w2, a0, a1, a2, g1, g2, k_k, k_a, r_k, lnx_g, lnx_b, v_lora):
    t, d = h.shape
    h_shift = jnp.concatenate([jnp.zeros((1, d), h.dtype), h[:-1]], axis=0)
    xm = rw_mix(h, h_shift, mix)
    wb = w_rkv.astype(BF16)
    r = matmul(xm, wb[0], lead=0)
    k = matmul(xm, wb[1], lead=1)
    v = matmul(xm, wb[2], lead=2)
    hw = matmul(xm, _pad_cols(w1).astype(BF16), lead=3, out_dtype=BF16, act=jnp.tanh)
    ha = matmul(xm, _pad_cols(a1).astype(BF16), lead=4, out_dtype=BF16)
    hg = matmul(xm, _pad_cols(g1).astype(BF16), lead=5, out_dtype=BF16, act=jax.nn.sigmoid)
    zero = jnp.zeros((d,), F32)
    rowp = jnp.stack([w0, a0, zero if v_lora is None else v_lora[0], k_k, k_a, r_k.reshape(d), lnx_g, lnx_b])
    if v_lora is None:
        out = rw_recurrence(r, k, v, None, hw, ha, hg, None, _pad_rows(w2).astype(BF16),
                            _pad_rows(a2).astype(BF16), _pad_rows(g2).astype(BF16), None, rowp)
        v_first = v
    else:
        hv = matmul(xm, _pad_cols(v_lora[1]).astype(BF16), lead=2, out_dtype=BF16)
        out = rw_recurrence(r, k, v, v_first, hw, ha, hg, hv, _pad_rows(w2).astype(BF16),
                            _pad_rows(a2).astype(BF16), _pad_rows(g2).astype(BF16),
                            _pad_rows(v_lora[2]).astype(BF16), rowp)
    return out, v_first


def _nsa_kernel(q_ref, gate_ref, kc_ref, vc_ref, ks_ref, vs_ref, *rest, n_heads_total):
    kw_refs = rest[0:5]
    vw_refs = rest[5:10]
    o_ref, m_s, l_s, acc_s = rest[10:14]
    g = pl.program_id(0)
    qb = pl.program_id(1)
    hpg, dk, qn = NSA_HPG, NSA_DK, Q_BLOCK
    scale = dk ** -0.5
    t0 = qb * qn
    hrows = [slice(h * qn, (h + 1) * qn) for h in range(hpg)]

    q2 = q_ref[...]
    qs = jnp.concatenate([q2[:, h * dk:(h + 1) * dk] for h in range(hpg)], axis=0)
    tq = t0 + lax.broadcasted_iota(jnp.int32, (qn, 1), 0)
    slopes = [jnp.exp2(-8.0 * jnp.full((1, 1), g * hpg + h + 1, jnp.int32).astype(F32) / n_heads_total)
              for h in range(hpg)]


    kc = kc_ref[0]
    n_cp = kc.shape[0]
    s_all = lax.dot_general(qs, kc, _NT, preferred_element_type=F32)
    pos_c = lax.broadcasted_iota(jnp.int32, (1, n_cp), 1) * CMP_STRIDE + (CMP_BLOCK - 1)
    d_c = (tq - pos_c).astype(F32)
    hide_c = jnp.where(d_c >= 0.0, 0.0, NEG_INF)
    p_grp = jnp.zeros((qn, n_cp), F32)
    p_heads = []
    for h in range(hpg):
        s = s_all[hrows[h]] * scale + (hide_c - slopes[h] * d_c)
        m = jnp.max(s, axis=-1, keepdims=True)
        p = jnp.exp(s - m)
        l = jnp.sum(p, axis=-1, keepdims=True)
        p = p * jnp.where(m > 0.5 * NEG_INF, 1.0 / l, 0.0)
        p_grp = p_grp + p
        p_heads.append(p.astype(BF16))
    o_c = jnp.dot(jnp.concatenate(p_heads, axis=0), vc_ref[0], preferred_element_type=F32)

    n_sel = (n_cp * CMP_STRIDE) // SEL_BLOCK
    ratio = SEL_BLOCK // CMP_STRIDE
    n_ov = CMP_BLOCK // CMP_STRIDE
    ci = lax.broadcasted_iota(jnp.int32, (n_cp, n_sel), 0)
    ji = lax.broadcasted_iota(jnp.int32, (n_cp, n_sel), 1)
    off = ci - ratio * ji
    cnt = jnp.maximum(jnp.minimum(jnp.minimum(off + n_ov, ratio - off), jnp.minimum(n_ov, ratio)), 0)
    cnt = cnt.astype(F32).astype(BF16)
    imp = sum(jnp.dot(part, cnt, preferred_element_type=F32) for part in _split3(p_grp))
    cur = tq // SEL_BLOCK
    sj = lax.broadcasted_iota(jnp.int32, (qn, n_sel), 1)
    forced = (sj == 0) | (sj == cur) | (sj == cur - 1)
    work = jnp.where(sj > cur, -1.0, jnp.where(forced, FORCE_SCORE, imp))
    sel = jnp.zeros((qn, n_sel), F32)
    sjf = sj.astype(F32)
    for _ in range(min(SEL_TOP_N, n_sel)):
        mx = jnp.max(work, axis=-1, keepdims=True)
        first = jnp.min(jnp.where(work == mx, sjf, float(n_sel)), axis=-1, keepdims=True)
        pick = sjf == first
        sel = jnp.where(pick, 1.0, sel)
        work = jnp.where(pick, -2.0, work)
    sel_b = sel.astype(BF16)

    m_s[...] = jnp.full_like(m_s, NEG_INF)
    l_s[...] = jnp.zeros_like(l_s)
    acc_s[...] = jnp.zeros_like(acc_s)
    n_keys = ks_ref.shape[0]
    tk = min(SEL_TILE, n_keys)
    bpt = tk // SEL_BLOCK

    def sweep(i, diagonal):
        k0 = pl.multiple_of(i * tk, tk)
        kt = ks_ref[pl.ds(k0, tk), :]
        vt = vs_ref[pl.ds(k0, tk), :]
        st = lax.dot_general(qs, kt, _NT, preferred_element_type=F32)
        bi = lax.broadcasted_iota(jnp.int32, (n_sel, tk), 0)
        ki = lax.broadcasted_iota(jnp.int32, (n_sel, tk), 1)
        expand = (bi == i * bpt + ki // SEL_BLOCK).astype(F32).astype(BF16)
        chosen = jnp.dot(sel_b, expand, preferred_element_type=F32)
        d_s = (tq - (k0 + lax.broadcasted_iota(jnp.int32, (1, tk), 1))).astype(F32)
        hide = (chosen - 1.0) * (-NEG_INF)
        if diagonal:
            hide = jnp.where(d_s >= 0.0, hide, NEG_INF)
        p_heads, alphas = [], []
        for h in range(hpg):
            s = st[hrows[h]] * scale + (hide - slopes[h] * d_s)
            m_old = m_s[hrows[h], :]
            m_new = jnp.maximum(m_old, jnp.max(s, axis=-1, keepdims=True))
            alpha = jnp.exp(m_old - m_new)
            p = jnp.exp(s - m_new)
            l_s[hrows[h], :] = alpha * l_s[hrows[h], :] + jnp.sum(p, axis=-1, keepdims=True)
            m_s[hrows[h], :] = m_new
            alphas.append(alpha)
            p_heads.append(p.astype(BF16))
        pv = jnp.dot(jnp.concatenate(p_heads, axis=0), vt, preferred_element_type=F32)
        acc_s[...] = jnp.concatenate(alphas, axis=0) * acc_s[...] + pv

    n_below = t0 // tk

    def below(i, carry):
        sweep(i, False)
        return carry

    lax.fori_loop(0, n_below, below, 0)
    sweep(n_below, True)
    o_s = acc_s[...] * (1.0 / l_s[...])

    kw = jnp.concatenate([r[...] for r in kw_refs], axis=0)
    vw = jnp.concatenate([r[...] for r in vw_refs], axis=0)
    sw = lax.dot_general(qs, kw, _NT, preferred_element_type=F32)
    pos_w = t0 - WINDOW + lax.broadcasted_iota(jnp.int32, (1, WINDOW + qn), 1)
    d_wi = tq - pos_w
    d_w = d_wi.astype(F32)
    hide_w = jnp.where((d_wi >= 0) & (d_wi < WINDOW) & (pos_w >= 0), 0.0, NEG_INF)
    p_heads, inv_l = [], []
    for h in range(hpg):
        s = sw[hrows[h]] * scale + (hide_w - slopes[h] * d_w)
        p = jnp.exp(s - jnp.max(s, axis=-1, keepdims=True))
        inv_l.append(1.0 / jnp.sum(p, axis=-1, keepdims=True))
        p_heads.append(p.astype(BF16))
    o_w = jnp.dot(jnp.concatenate(p_heads, axis=0), vw, preferred_element_type=F32) * jnp.concatenate(inv_l, axis=0)

    gate = gate_ref[...]
    for h in range(hpg):
        g_c, g_s, g_w = (gate[:, b * hpg + h:b * hpg + h + 1] for b in range(3))
        o = g_c * o_c[hrows[h]] + g_s * o_s[hrows[h]] + g_w * o_w[hrows[h]]
        o_ref[:, h * dk:(h + 1) * dk] = o.astype(o_ref.dtype)


def nsa_attention_core(q, gate, k_cmp, v_cmp, kvb, n_groups):
    t, d = q.shape
    gq = NSA_HPG * NSA_DK
    n_qb = t // Q_BLOCK
    nwb = WINDOW // Q_BLOCK
    n_cp = k_cmp.shape[1]
    full = lambda j: pl.BlockSpec((t, NSA_DK), lambda g, qb, j=j: (0, j * n_groups + g))

    def win(j, i):
        return pl.BlockSpec((Q_BLOCK, NSA_DK), lambda g, qb, j=j, i=i: (jnp.maximum(qb - nwb + i, 0), j * n_groups + g))

    in_specs = [pl.BlockSpec((Q_BLOCK, gq), lambda g, qb: (qb, g)),
                pl.BlockSpec((Q_BLOCK, LANES), lambda g, qb: (qb, g)),
                pl.BlockSpec((1, n_cp, NSA_DK), lambda g, qb: (g, 0, 0)),
                pl.BlockSpec((1, n_cp, NSA_DK), lambda g, qb: (g, 0, 0)),
                full(2), full(3)]
    in_specs += [win(4, i) for i in range(nwb + 1)] + [win(5, i) for i in range(nwb + 1)]
    rows = NSA_HPG * Q_BLOCK
    return pl.pallas_call(
        functools.partial(_nsa_kernel, n_heads_total=n_groups * NSA_HPG),
        grid=(n_groups, n_qb),
        in_specs=in_specs,
        out_specs=pl.BlockSpec((Q_BLOCK, gq), lambda g, qb: (qb, g)),
        out_shape=jax.ShapeDtypeStruct((t, d), BF16),
        scratch_shapes=[pltpu.VMEM((rows, 1), F32), pltpu.VMEM((rows, 1), F32), pltpu.VMEM((rows, NSA_DK), F32)],
        compiler_params=_cparams(("parallel", "arbitrary")),
        name="nsa_attention",
    )(q, gate, k_cmp, v_cmp, kvb, kvb, *([kvb] * (2 * (nwb + 1))))


def nsa_shared_kv(hb, w_kv, cmp_pos, cmp_w1, cmp_b1, cmp_w2):
    t = hb.shape[0]
    n_groups = w_kv.shape[1] // (6 * NSA_DK)
    kv = matmul(hb, w_kv.astype(BF16))
    n_chunk = t // CMP_STRIDE
    cmp = []
    for j in range(2):
        z = kv[:, j * n_groups * NSA_DK:(j + 1) * n_groups * NSA_DK]
        ch = z.reshape(n_chunk, CMP_STRIDE, n_groups, NSA_DK).transpose(2, 0, 1, 3).reshape(n_groups, n_chunk, CMP_STRIDE * NSA_DK)
        nxt = jnp.concatenate([ch[:, 1:], jnp.zeros_like(ch[:, :1])], axis=1)
        flat = jnp.concatenate([ch, nxt], axis=-1).reshape(n_groups * n_chunk, CMP_BLOCK * NSA_DK)
        hid = matmul(flat, cmp_w1[j].astype(BF16), out_dtype=BF16, act=jax.nn.silu,
                     a_bias=cmp_pos[j].reshape(-1), bias=cmp_b1[j])
        out = matmul(hid, cmp_w2[j].astype(BF16))
        cmp.append(out.reshape(n_groups, n_chunk, NSA_DK).astype(BF16))
    return cmp[0], cmp[1], kv.astype(BF16)


def nsa_layer(hb, shared, w_qg, b_g):
    k_cmp, v_cmp, kvb = shared
    n_groups = k_cmp.shape[0]
    n_heads = n_groups * NSA_HPG
    dq = n_heads * NSA_DK
    q = matmul(hb, w_qg[:, :dq].astype(BF16), out_dtype=BF16)
    pad = LANES - 3 * NSA_HPG
    w_g = w_qg[:, dq:].reshape(-1, n_groups, NSA_HPG, 3).transpose(0, 1, 3, 2).reshape(-1, n_groups, 3 * NSA_HPG)
    w_g = jnp.pad(w_g, ((0, 0), (0, 0), (0, pad))).reshape(-1, n_groups * LANES)
    b_gp = b_g.reshape(n_groups, NSA_HPG, 3).transpose(0, 2, 1).reshape(n_groups, 3 * NSA_HPG)
    b_gp = jnp.pad(b_gp, ((0, 0), (0, pad))).reshape(-1)
    gate = matmul(hb, w_g.astype(BF16), act=jax.nn.sigmoid, bias=b_gp)
    return nsa_attention_core(q, gate, k_cmp, v_cmp, kvb, n_groups)


def kernel(x, p, ln_g, ln_b, ffn1_w13, ffn1_w2, ffn2_w13, ffn2_w2, ple_w, ple_gate_down, ple_gate_up, rw_mix, rw_w_rkv, rw_w_o, rw_w0, rw_w1, rw_w2, rw_a0, rw_a1, rw_a2, rw_g1, rw_g2, rw_k_k, rw_k_a, rw_r_k, rw_lnx_g, rw_lnx_b, rw_v0, rw_v1, rw_v2, nsa_w_kv, cmp_pos, cmp_w1, cmp_b1, cmp_w2, nsa_w_qg, nsa_b_g, nsa_w_o):
    batch, t, d = x.shape
    depth = ln_g.shape[0]
    n_a = rw_mix.shape[0]
    outs = []
    for bi in range(batch):
        h = x[bi]
        hb = h.astype(BF16)
        v_first = None
        shared = None
        for i in range(depth):
            mid = swiglu_up(hb, ffn1_w13[i].astype(BF16))
            h, hb = mm_res_ln(mid, ffn1_w2[i].astype(BF16), h, ln_g[i, 0], ln_b[i, 0], scale=0.5)
            if i < n_a:
                v_lora = None if i == 0 else (rw_v0[i - 1], rw_v1[i - 1], rw_v2[i - 1])
                mix_pre, v_first = rwkv7_time_mix(
                    h, v_first, rw_mix[i], rw_w_rkv[i], rw_w_o[i], rw_w0[i], rw_w1[i], rw_w2[i], rw_a0[i],
                    rw_a1[i], rw_a2[i], rw_g1[i], rw_g2[i], rw_k_k[i], rw_k_a[i], rw_r_k[i], rw_lnx_g[i],
                    rw_lnx_b[i], v_lora)
                w_o = rw_w_o[i]
            else:
                j = i - n_a
                mix_pre = nsa_layer(hb, shared, nsa_w_qg[j], nsa_b_g[j])
                w_o = nsa_w_o[j]
            h, hb = mm_res_ln(mix_pre, w_o.astype(BF16), h, ln_g[i, 1], ln_b[i, 1], scale=1.0)
            mid = swiglu_up(hb, ffn2_w13[i].astype(BF16))
            h, hb = mm_res_ln(mid, ffn2_w2[i].astype(BF16), h, ln_g[i, 2], ln_b[i, 2], scale=0.5)
            tdown = matmul(hb, ple_gate_down[i].astype(BF16), out_dtype=BF16)
            h, hb = ple_ln(tdown, p[i, bi], ple_gate_up[i].astype(BF16), ple_w[i].astype(BF16), h,
                           ln_g[i, 3], ln_b[i, 3])
            if i == n_a - 1:
                shared = nsa_shared_kv(hb, nsa_w_kv, cmp_pos, cmp_w1, cmp_b1, cmp_w2)
        outs.append(h)
    return jnp.stack(outs, axis=0)
```

```python
import functools
import math

import jax
import jax.numpy as jnp
from jax import lax
from jax.experimental import pallas as pl
from jax.experimental.pallas import tpu as pltpu

F32 = jnp.float32
BF16 = jnp.bfloat16

V7X_VMEM_BYTES = 64 * 1024 * 1024
VMEM_LIMIT = V7X_VMEM_BYTES - 8 * 1024 * 1024
LANES = 128

LN_EPS = 1e-5
LN_SLAB = 64
DEPTH_TOTAL = 4
ALPHA = (2.0 * DEPTH_TOTAL) ** 0.25

RW_HEAD = 64
RW_GN_EPS = 64e-5
RW_CHUNK = 64

NSA_DK = 128
NSA_HPG = 8
CMP_BLOCK = 32
CMP_STRIDE = 16
SEL_BLOCK = 64
SEL_TOP_N = 16
WINDOW = 512
Q_BLOCK = 128
FORCE_SCORE = 1e4
NEG_INF = -1e30
SEL_TILE = 512

_NN = (((1,), (0,)), ((), ()))
_NT = (((1,), (1,)), ((), ()))
_TN = (((0,), (0,)), ((), ()))


def _cparams(sem):
    return pltpu.CompilerParams(dimension_semantics=sem, vmem_limit_bytes=VMEM_LIMIT)


def _pick(n, pref):
    if n <= pref:
        return n
    t = pref
    while n % t:
        t //= 2
    return t


def _mm_kernel(*refs, nk, act, has_abias, has_bias):
    a_ref, b_ref = refs[0], refs[1]
    i = 2
    abias_ref = bias_ref = None
    if has_abias:
        abias_ref = refs[i]; i += 1
    if has_bias:
        bias_ref = refs[i]; i += 1
    o_ref, acc_ref = refs[i], refs[i + 1]
    k = pl.program_id(2)
    a = a_ref[...]
    if has_abias:
        a = a.astype(F32) + abias_ref[...]
    part = jnp.dot(a.astype(BF16), b_ref[...], preferred_element_type=F32)

    @pl.when(k == 0)
    def _():
        acc_ref[...] = part

    @pl.when(k > 0)
    def _():
        acc_ref[...] += part

    @pl.when(k == nk - 1)
    def _():
        z = acc_ref[...]
        if has_bias:
            z = z + bias_ref[...]
        if act is not None:
            z = act(z)
        o_ref[...] = z.astype(o_ref.dtype)


def matmul(a, b, *, lead=None, out_dtype=F32, act=None, a_bias=None, bias=None, tm=1024, tn=1024, tk=1024):
    m, kd = a.shape[-2:]
    _, n = b.shape
    tm, tn, tk = _pick(m, tm), _pick(n, tn), _pick(kd, tk)
    nk = kd // tk
    if lead is None:
        a_spec = pl.BlockSpec((tm, tk), lambda i, j, k: (i, k))
    else:
        a_spec = pl.BlockSpec((None, tm, tk), lambda i, j, k: (lead, i, k))
    in_specs = [a_spec, pl.BlockSpec((tk, tn), lambda i, j, k: (k, j))]
    args = [a, b]
    if a_bias is not None:
        in_specs.append(pl.BlockSpec((1, tk), lambda i, j, k: (0, k)))
        args.append(a_bias.reshape(1, kd).astype(F32))
    if bias is not None:
        in_specs.append(pl.BlockSpec((1, tn), lambda i, j, k: (0, j)))
        args.append(bias.reshape(1, n).astype(F32))
    return pl.pallas_call(
        functools.partial(_mm_kernel, nk=nk, act=act, has_abias=a_bias is not None, has_bias=bias is not None),
        grid=(m // tm, n // tn, nk),
        in_specs=in_specs,
        out_specs=pl.BlockSpec((tm, tn), lambda i, j, k: (i, j)),
        out_shape=jax.ShapeDtypeStruct((m, n), out_dtype),
        scratch_shapes=[pltpu.VMEM((tm, tn), F32)],
        compiler_params=_cparams(("parallel", "parallel", "arbitrary")),
        name="matmul",
    )(*args)


def _swiglu_kernel(a_ref, b1_ref, b3_ref, o_ref, acc1_ref, acc3_ref, *, nk):
    k = pl.program_id(2)
    a = a_ref[...].astype(BF16)
    p1 = jnp.dot(a, b1_ref[...], preferred_element_type=F32)
    p3 = jnp.dot(a, b3_ref[...], preferred_element_type=F32)

    @pl.when(k == 0)
    def _():
        acc1_ref[...] = p1
        acc3_ref[...] = p3

    @pl.when(k > 0)
    def _():
        acc1_ref[...] += p1
        acc3_ref[...] += p3

    @pl.when(k == nk - 1)
    def _():
        g = acc1_ref[...]
        o_ref[...] = (g * jax.nn.sigmoid(g) * acc3_ref[...]).astype(o_ref.dtype)


def swiglu_up(a, w13, *, tm=1024, tn=1024, tk=1024):
    m, kd = a.shape
    f = w13.shape[1] // 2
    tm, tn, tk = _pick(m, tm), _pick(f, tn), _pick(kd, tk)
    nk, nj = kd // tk, f // tn
    return pl.pallas_call(
        functools.partial(_swiglu_kernel, nk=nk),
        grid=(m // tm, nj, nk),
        in_specs=[pl.BlockSpec((tm, tk), lambda i, j, k: (i, k)),
                  pl.BlockSpec((tk, tn), lambda i, j, k: (k, j)),
                  pl.BlockSpec((tk, tn), lambda i, j, k: (k, j + nj))],
        out_specs=pl.BlockSpec((tm, tn), lambda i, j, k: (i, j)),
        out_shape=jax.ShapeDtypeStruct((m, f), BF16),
        scratch_shapes=[pltpu.VMEM((tm, tn), F32), pltpu.VMEM((tm, tn), F32)],
        compiler_params=_cparams(("parallel", "parallel", "arbitrary")),
        name="swiglu_up",
    )(a, w13, w13)


def _deepnorm(z, g, b):
    mu = jnp.mean(z, axis=-1, keepdims=True)
    zc = z - mu
    var = jnp.mean(zc * zc, axis=-1, keepdims=True)
    return zc * lax.rsqrt(var + LN_EPS) * g + b


def _mm_res_ln_kernel(a_ref, b_ref, h_ref, g_ref, beta_ref, o_ref, ob_ref, *, nk, scale):
    k = pl.program_id(1)
    part = jnp.dot(a_ref[...].astype(BF16), b_ref[...], preferred_element_type=F32)

    @pl.when(k == 0)
    def _():
        o_ref[...] = part

    @pl.when(k > 0)
    def _():
        o_ref[...] += part

    @pl.when(k == nk - 1)
    def _():
        def slab(i, carry):
            rows = pl.ds(pl.multiple_of(i * LN_SLAB, LN_SLAB), LN_SLAB)
            y = _deepnorm(ALPHA * h_ref[rows, :] + scale * o_ref[rows, :], g_ref[...], beta_ref[...])
            o_ref[rows, :] = y
            ob_ref[rows, :] = y.astype(BF16)
            return carry

        lax.fori_loop(0, o_ref.shape[0] // LN_SLAB, slab, 0)


def mm_res_ln(a, b, h, g, beta, *, scale, tm=512, tk=512):
    m, kd = a.shape
    n = b.shape[1]
    tm, tk = _pick(m, tm), _pick(kd, tk)
    assert tm % LN_SLAB == 0, (tm, LN_SLAB)
    nk = kd // tk
    return pl.pallas_call(
        functools.partial(_mm_res_ln_kernel, nk=nk, scale=scale),
        grid=(m // tm, nk),
        in_specs=[pl.BlockSpec((tm, tk), lambda i, k: (i, k)),
                  pl.BlockSpec((tk, n), lambda i, k: (k, 0)),
                  pl.BlockSpec((tm, n), lambda i, k: (i, 0), pipeline_mode=pl.Buffered(1)),
                  pl.BlockSpec((1, n), lambda i, k: (0, 0)),
                  pl.BlockSpec((1, n), lambda i, k: (0, 0))],
        out_specs=[pl.BlockSpec((tm, n), lambda i, k: (i, 0)),
                   pl.BlockSpec((tm, n), lambda i, k: (i, 0))],
        out_shape=[jax.ShapeDtypeStruct((m, n), F32), jax.ShapeDtypeStruct((m, n), BF16)],
        compiler_params=_cparams(("parallel", "arbitrary")),
        name="mm_res_ln",
    )(a, b, h, g.reshape(1, n), beta.reshape(1, n))


def _ple_kernel(t_ref, p_ref, gu_ref, pw_ref, h_ref, g_ref, beta_ref, o_ref, ob_ref):
    gate = jax.nn.sigmoid(jnp.dot(t_ref[...], gu_ref[...], preferred_element_type=F32))
    emb = jnp.dot(p_ref[...].astype(BF16), pw_ref[...], preferred_element_type=F32)
    y = _deepnorm(ALPHA * h_ref[...] + gate * emb, g_ref[...], beta_ref[...])
    o_ref[...] = y
    ob_ref[...] = y.astype(BF16)


def ple_ln(t, p, gu, pw, h, g, beta, *, tm=256):
    m, n = h.shape
    e = t.shape[1]
    tm = _pick(m, tm)
    row = lambda i: (i, 0)
    fixed = lambda i: (0, 0)
    return pl.pallas_call(
        _ple_kernel,
        grid=(m // tm,),
        in_specs=[pl.BlockSpec((tm, e), row), pl.BlockSpec((tm, e), row),
                  pl.BlockSpec((e, n), fixed), pl.BlockSpec((e, n), fixed),
                  pl.BlockSpec((tm, n), row), pl.BlockSpec((1, n), fixed), pl.BlockSpec((1, n), fixed)],
        out_specs=[pl.BlockSpec((tm, n), row), pl.BlockSpec((tm, n), row)],
        out_shape=[jax.ShapeDtypeStruct((m, n), F32), jax.ShapeDtypeStruct((m, n), BF16)],
        compiler_params=_cparams(("parallel",)),
        name="ple_ln",
    )(t, p, gu, pw, h, g.reshape(1, n), beta.reshape(1, n))


def _split2(x):
    hi = x.astype(BF16)
    return hi, (x - hi.astype(F32)).astype(BF16)


def _mm3(a, b, dims=_NN):
    dg = lambda x, y: lax.dot_general(x, y, dims, preferred_element_type=F32)
    return dg(a[0], b[0]) + (dg(a[0], b[1]) + dg(a[1], b[0]))


def _mm1(a, b, dims=_NN):
    return lax.dot_general(a, b, dims, preferred_element_type=F32)


def _split3(x):
    hi = x.astype(BF16)
    r1 = x - hi.astype(F32)
    mid = r1.astype(BF16)
    return hi, mid, (r1 - mid.astype(F32)).astype(BF16)


def _rw_mix_kernel(h_ref, hs_ref, mix_ref, o_ref):
    x = h_ref[...]
    xx = hs_ref[...] - x
    for c in range(6):
        o_ref[c] = (x + xx * mix_ref[c:c + 1, :]).astype(BF16)


def rw_mix(h, h_shift, mix, *, tm=256, tn=1024):
    m, n = h.shape
    tm, tn = _pick(m, tm), _pick(n, tn)
    return pl.pallas_call(
        _rw_mix_kernel,
        grid=(m // tm, n // tn),
        in_specs=[pl.BlockSpec((tm, tn), lambda i, j: (i, j)),
                  pl.BlockSpec((tm, tn), lambda i, j: (i, j)),
                  pl.BlockSpec((8, tn), lambda i, j: (0, j))],
        out_specs=pl.BlockSpec((6, tm, tn), lambda i, j: (0, i, j)),
        out_shape=jax.ShapeDtypeStruct((6, m, n), BF16),
        compiler_params=_cparams(("parallel", "parallel")),
        name="rw_mix",
    )(h, h_shift, jnp.pad(mix, ((0, 2), (0, 0))))


_RP_W0, _RP_A0, _RP_V0, _RP_KK, _RP_KA, _RP_RK, _RP_LNG, _RP_LNB = range(8)


def _each(f, *lists):
    return [f(*xs) for xs in zip(*lists)]


def _rw_chunk_local(a_t, r_t, b_h, k_h, v, p_last, strict, incl, eye):
    sp = lambda xs: _each(_split2, xs)
    rnd = lambda xs: _each(lambda z: z.astype(BF16), xs)
    nt = lambda a, b: _mm3(a, b, _NT)
    tn = lambda a, b: _mm1(a, b, _TN)
    at, rt, bh, kh, vs = sp(a_t), sp(r_t), sp(b_h), sp(k_h), sp(v)
    ath, bhh, khh, vsh = ([z[0] for z in zs] for zs in (at, bh, kh, vs))
    m_ab = _each(lambda z: jnp.where(strict, z, 0.0), _each(nt, at, bh))
    m_ak = _each(lambda z: jnp.where(strict, z, 0.0), _each(nt, at, kh))
    m_rb = _each(lambda z: jnp.where(incl, z, 0.0), _each(nt, rt, bh))
    m_rk = _each(lambda z: jnp.where(incl, z, 0.0), _each(nt, rt, kh))
    x = _each(lambda z: eye + z, m_ab)
    pw = m_ab
    for _ in range(max(1, int(math.ceil(math.log2(a_t[0].shape[0]))) - 1)):
        ps = rnd(pw)
        pw = _each(_mm1, ps, ps)
        x = _each(lambda z, d: z + d, x, _each(_mm1, rnd(x), rnd(pw)))
    xs = rnd(x)
    a_x = _each(_mm1, xs, ath)
    u0 = _each(_mm1, xs, rnd(_each(_mm1, rnd(m_ak), vsh)))
    axs, u0s, rbs = sp(a_x), sp(u0), sp(m_rb)
    r_hat = _each(lambda z, d: z + d, r_t, _each(_mm3, rbs, axs))
    y0 = _each(lambda z, d: z + d, _each(_mm3, rbs, u0s), _each(_mm3, sp(m_rk), vs))
    axh, u0h = [z[0] for z in axs], [z[0] for z in u0s]
    t_m = _each(lambda z, p: (eye + z) * p, _each(tn, axh, bhh), p_last)
    s_add = _each(lambda z, d, p: (z + d) * p, _each(tn, u0h, bhh), _each(tn, vsh, khh), p_last)
    return r_hat, y0, t_m, s_add


def _rw_rec_kernel(*refs, has_vres, n_chunks):
    if has_vres:
        (r_ref, k_ref, v_ref, vf_ref, hw_ref, ha_ref, hg_ref, hv_ref, w2_ref, a2_ref, g2_ref, v2_ref,
         rp_ref, o_ref, s_ref, at_s, rt_s, bh_s, kh_s, v_s, g_s, bon_s, rh_s, y0_s, tm_s, sa_s) = refs
    else:
        (r_ref, k_ref, v_ref, hw_ref, ha_ref, hg_ref, w2_ref, a2_ref, g2_ref,
         rp_ref, o_ref, s_ref, at_s, rt_s, bh_s, kh_s, v_s, g_s, bon_s, rh_s, y0_s, tm_s, sa_s) = refs
    n = RW_HEAD
    c = RW_CHUNK
    heads = LANES // n

    @pl.when(pl.program_id(1) == 0)
    def _():
        s_ref[...] = jnp.zeros_like(s_ref)

    rp = rp_ref[...]
    row = lambda i: rp[i:i + 1, :]
    lane = lax.broadcasted_iota(jnp.int32, (1, LANES), 1)

    def per_head_sum(z):
        out = jnp.zeros_like(z)
        for j in range(heads):
            mine = (lane >= j * n) & (lane < (j + 1) * n)
            out = jnp.where(mine, jnp.sum(jnp.where(mine, z, 0.0), axis=-1, keepdims=True), out)
        return out

    zw = row(_RP_W0) + jnp.dot(hw_ref[...], w2_ref[...], preferred_element_type=F32)
    nz = -zw
    softplus = jnp.maximum(nz, 0.0) + jnp.log1p(jnp.exp(-jnp.abs(nz)))
    lw = -jnp.exp(-softplus - 0.5)
    a = jax.nn.sigmoid(row(_RP_A0) + jnp.dot(ha_ref[...], a2_ref[...], preferred_element_type=F32))
    g_s[...] = jnp.dot(hg_ref[...], g2_ref[...], preferred_element_type=F32)
    r = r_ref[...]
    k = k_ref[...]
    v = v_ref[...]
    if has_vres:
        mixv = jax.nn.sigmoid(row(_RP_V0) + jnp.dot(hv_ref[...], v2_ref[...], preferred_element_type=F32))
        v = v + (vf_ref[...] - v) * mixv
    v_s[...] = v
    kk = k * row(_RP_KK)
    kk = kk * lax.rsqrt(jnp.maximum(per_head_sum(kk * kk), 1e-24))
    k = k * (1.0 + (a - 1.0) * row(_RP_KA))
    bon_s[...] = per_head_sum(r * k * row(_RP_RK)) * v

    ti = lax.broadcasted_iota(jnp.int32, (c, c), 0)
    si = lax.broadcasted_iota(jnp.int32, (c, c), 1)
    strict = si < ti
    incl = si <= ti
    tri_incl = incl.astype(BF16)
    eye = (si == ti).astype(F32)

    p_last = []
    for ci in range(n_chunks):
        rows = slice(ci * c, (ci + 1) * c)
        lw_c = lw[rows]
        cum = sum(jnp.dot(tri_incl, part, preferred_element_type=F32) for part in _split3(lw_c))
        p_in = jnp.exp(cum)
        p_inv = jnp.exp(-cum)
        at_s[rows, :] = -kk[rows] * jnp.exp(cum - lw_c)
        rt_s[rows, :] = r[rows] * p_in
        bh_s[rows, :] = kk[rows] * a[rows] * p_inv
        kh_s[rows, :] = k[rows] * p_inv
        p_last.append(p_in[c - 1:c, :])

    where = [(slice(ci * c, (ci + 1) * c), slice(j * n, (j + 1) * n)) for ci in range(n_chunks) for j in range(heads)]
    r_hat, y0, t_m, s_add = _rw_chunk_local(
        [at_s[w] for w in where], [rt_s[w] for w in where], [bh_s[w] for w in where], [kh_s[w] for w in where],
        [v_s[w] for w in where], [p_last[ci][:, j * n:(j + 1) * n] for ci in range(n_chunks) for j in range(heads)],
        strict, incl, eye)
    for i, w in enumerate(where):
        rh_s[w] = r_hat[i]
        y0_s[w] = y0[i]
        tm_s[i] = t_m[i]
        sa_s[i] = s_add[i]

    s = [s_ref[j] for j in range(heads)]
    for ci in range(n_chunks):
        rows = slice(ci * c, (ci + 1) * c)
        for j in range(heads):
            ln = slice(j * n, (j + 1) * n)
            ss = _split2(s[j])
            y = _mm3(_split2(rh_s[rows, ln]), ss, _NT) + y0_s[rows, ln]
            s[j] = _mm3(ss, _split2(tm_s[ci * heads + j])) + sa_s[ci * heads + j]
            mu = jnp.mean(y, axis=-1, keepdims=True)
            yc = y - mu
            var = jnp.mean(yc * yc, axis=-1, keepdims=True)
            yn = yc * lax.rsqrt(var + RW_GN_EPS) * row(_RP_LNG)[:, ln] + row(_RP_LNB)[:, ln]
            o_ref[rows, ln] = ((yn + bon_s[rows, ln]) * g_s[rows, ln]).astype(o_ref.dtype)
    for j in range(heads):
        s_ref[j] = s[j]


def rw_recurrence(r, k, v, v_first, hw, ha, hg, hv, w2, a2, g2, v2, rowp, *, tb=512):
    t, d = r.shape
    tb = _pick(t, tb)
    has_vres = v_first is not None
    tile = pl.BlockSpec((tb, LANES), lambda hb, i: (i, hb))
    lora = lambda w: pl.BlockSpec((tb, w), lambda hb, i: (i, 0))
    wcol = lambda w: pl.BlockSpec((w, LANES), lambda hb, i: (0, hb))
    lw_, la_, lg_ = hw.shape[1], ha.shape[1], hg.shape[1]
    if has_vres:
        args = [r, k, v, v_first, hw, ha, hg, hv, w2, a2, g2, v2, rowp]
        in_specs = [tile, tile, tile, tile, lora(lw_), lora(la_), lora(lg_), lora(hv.shape[1]),
                    wcol(lw_), wcol(la_), wcol(lg_), wcol(hv.shape[1]), wcol(8)]
    else:
        args = [r, k, v, hw, ha, hg, w2, a2, g2, rowp]
        in_specs = [tile, tile, tile, lora(lw_), lora(la_), lora(lg_), wcol(lw_), wcol(la_), wcol(lg_), wcol(8)]
    heads = LANES // RW_HEAD
    n_chunks = tb // RW_CHUNK
    per_chunk = pltpu.VMEM((n_chunks * heads, RW_HEAD, RW_HEAD), F32)
    return pl.pallas_call(
        functools.partial(_rw_rec_kernel, has_vres=has_vres, n_chunks=n_chunks),
        grid=(d // LANES, t // tb),
        in_specs=in_specs,
        out_specs=tile,
        out_shape=jax.ShapeDtypeStruct((t, d), BF16),
        scratch_shapes=[pltpu.VMEM((heads, RW_HEAD, RW_HEAD), F32)]
        + [pltpu.VMEM((tb, LANES), F32)] * 9 + [per_chunk, per_chunk],
        compiler_params=_cparams(("parallel", "arbitrary")),
        name="rw_recurrence",
    )(*args)


def _pad_cols(w, mult=LANES):
    p = (-w.shape[-1]) % mult
    return jnp.pad(w, ((0, 0), (0, p))) if p else w


def _pad_rows(w, mult=LANES):
    p = (-w.shape[0]) % mult
    return jnp.pad(w, ((0, p), (0, 0))) if p else w


def rwkv7_time_mix(h, v_first, mix, w_rkv, w_o, w0, w1, w2, a0, a1, a2, g1, g2, k_k, k_a, r_k, lnx_g, lnx_b, v_lora):
    t, d = h.shape
    h_shift = jnp.concatenate([jnp.zeros((1, d), h.dtype), h[:-1]], axis=0)
    xm = rw_mix(h, h_shift, mix)
    wb = w_rkv.astype(BF16)
    r = matmul(xm, wb[0], lead=0)
    k = matmul(xm, wb[1], lead=1)
    v = matmul(xm, wb[2], lead=2)
    hw = matmul(xm, _pad_cols(w1).astype(BF16), lead=3, out_dtype=BF16, act=jnp.tanh)
    ha = matmul(xm, _pad_cols(a1).astype(BF16), lead=4, out_dtype=BF16)
    hg = matmul(xm, _pad_cols(g1).astype(BF16), lead=5, out_dtype=BF16, act=jax.nn.sigmoid)
    zero = jnp.zeros((d,), F32)
    rowp = jnp.stack([w0, a0, zero if v_lora is None else v_lora[0], k_k, k_a, r_k.reshape(d), lnx_g, lnx_b])
    if v_lora is None:
        out = rw_recurrence(r, k, v, None, hw, ha, hg, None, _pad_rows(w2).astype(BF16),
                            _pad_rows(a2).astype(BF16), _pad_rows(g2).astype(BF16), None, rowp)
        v_first = v
    else:
        hv = matmul(xm, _pad_cols(v_lora[1]).astype(BF16), lead=2, out_dtype=BF16)
        out = rw_recurrence(r, k, v, v_first, hw, ha, hg, hv, _pad_rows(w2).astype(BF16),
                            _pad_rows(a2).astype(BF16), _pad_rows(g2).astype(BF16),
                            _pad_rows(v_lora[2]).astype(BF16), rowp)
    return out, v_first


def _nsa_kernel(q_ref, gate_ref, kc_ref, vc_ref, ks_ref, vs_ref, *rest, n_heads_total):
    kw_refs = rest[0:5]
    vw_refs = rest[5:10]
    o_ref, m_s, l_s, acc_s = rest[10:14]
    g = pl.program_id(0)
    qb = pl.program_id(1)
    hpg, dk, qn = NSA_HPG, NSA_DK, Q_BLOCK
    scale = dk ** -0.5
    t0 = qb * qn
    hrows = [slice(h * qn, (h + 1) * qn) for h in range(hpg)]

    q2 = q_ref[...]
    qs = jnp.concatenate([q2[:, h * dk:(h + 1) * dk] for h in range(hpg)], axis=0)
    tq = t0 + lax.broadcasted_iota(jnp.int32, (qn, 1), 0)
    slopes = [jnp.exp2(-8.0 * jnp.full((1, 1), g * hpg + h + 1, jnp.int32).astype(F32) / n_heads_total)
              for h in range(hpg)]


    kc = kc_ref[0]
    n_cp = kc.shape[0]
    s_all = lax.dot_general(qs, kc, _NT, preferred_element_type=F32)
    pos_c = lax.broadcasted_iota(jnp.int32, (1, n_cp), 1) * CMP_STRIDE + (CMP_BLOCK - 1)
    d_c = (tq - pos_c).astype(F32)
    hide_c = jnp.where(d_c >= 0.0, 0.0, NEG_INF)
    p_grp = jnp.zeros((qn, n_cp), F32)
    p_heads = []
    for h in range(hpg):
        s = s_all[hrows[h]] * scale + (hide_c - slopes[h] * d_c)
        m = jnp.max(s, axis=-1, keepdims=True)
        p = jnp.exp(s - m)
        l = jnp.sum(p, axis=-1, keepdims=True)
        p = p * jnp.where(m > 0.5 * NEG_INF, 1.0 / l, 0.0)
        p_grp = p_grp + p
        p_heads.append(p.astype(BF16))
    o_c = jnp.dot(jnp.concatenate(p_heads, axis=0), vc_ref[0], preferred_element_type=F32)

    n_sel = (n_cp * CMP_STRIDE) // SEL_BLOCK
    ratio = SEL_BLOCK // CMP_STRIDE
    n_ov = CMP_BLOCK // CMP_STRIDE
    ci = lax.broadcasted_iota(jnp.int32, (n_cp, n_sel), 0)
    ji = lax.broadcasted_iota(jnp.int32, (n_cp, n_sel), 1)
    off = ci - ratio * ji
    cnt = jnp.maximum(jnp.minimum(jnp.minimum(off + n_ov, ratio - off), jnp.minimum(n_ov, ratio)), 0)
    cnt = cnt.astype(F32).astype(BF16)
    imp = sum(jnp.dot(part, cnt, preferred_element_type=F32) for part in _split3(p_grp))
    cur = tq // SEL_BLOCK
    sj = lax.broadcasted_iota(jnp.int32, (qn, n_sel), 1)
    forced = (sj == 0) | (sj == cur) | (sj == cur - 1)
    work = jnp.where(sj > cur, -1.0, jnp.where(forced, FORCE_SCORE, imp))
    sel = jnp.zeros((qn, n_sel), F32)
    sjf = sj.astype(F32)
    for _ in range(min(SEL_TOP_N, n_sel)):
        mx = jnp.max(work, axis=-1, keepdims=True)
        first = jnp.min(jnp.where(work == mx, sjf, float(n_sel)), axis=-1, keepdims=True)
        pick = sjf == first
        sel = jnp.where(pick, 1.0, sel)
        work = jnp.where(pick, -2.0, work)
    sel_b = sel.astype(BF16)

    m_s[...] = jnp.full_like(m_s, NEG_INF)
    l_s[...] = jnp.zeros_like(l_s)
    acc_s[...] = jnp.zeros_like(acc_s)
    n_keys = ks_ref.shape[0]
    tk = min(SEL_TILE, n_keys)
    bpt = tk // SEL_BLOCK

    def sweep(i, diagonal):
        k0 = pl.multiple_of(i * tk, tk)
        kt = ks_ref[pl.ds(k0, tk), :]
        vt = vs_ref[pl.ds(k0, tk), :]
        st = lax.dot_general(qs, kt, _NT, preferred_element_type=F32)
        bi = lax.broadcasted_iota(jnp.int32, (n_sel, tk), 0)
        ki = lax.broadcasted_iota(jnp.int32, (n_sel, tk), 1)
        expand = (bi == i * bpt + ki // SEL_BLOCK).astype(F32).astype(BF16)
        chosen = jnp.dot(sel_b, expand, preferred_element_type=F32)
        d_s = (tq - (k0 + lax.broadcasted_iota(jnp.int32, (1, tk), 1))).astype(F32)
        hide = (chosen - 1.0) * (-NEG_INF)
        if diagonal:
            hide = jnp.where(d_s >= 0.0, hide, NEG_INF)
        p_heads, alphas = [], []
        for h in range(hpg):
            s = st[hrows[h]] * scale + (hide - slopes[h] * d_s)
            m_old = m_s[hrows[h], :]
            m_new = jnp.maximum(m_old, jnp.max(s, axis=-1, keepdims=True))
            alpha = jnp.exp(m_old - m_new)
            p = jnp.exp(s - m_new)
            l_s[hrows[h], :] = alpha * l_s[hrows[h], :] + jnp.sum(p, axis=-1, keepdims=True)
            m_s[hrows[h], :] = m_new
            alphas.append(alpha)
            p_heads.append(p.astype(BF16))
        pv = jnp.dot(jnp.concatenate(p_heads, axis=0), vt, preferred_element_type=F32)
        acc_s[...] = jnp.concatenate(alphas, axis=0) * acc_s[...] + pv

    n_below = t0 // tk
    block_used = jnp.max(sel, axis=0, keepdims=True)
    block_id = lax.broadcasted_iota(jnp.int32, (1, n_sel), 1)

    def below(i, carry):
        in_tile = (block_id >= i * bpt) & (block_id < (i + 1) * bpt)
        used = jnp.max(jnp.where(in_tile, block_used, 0.0))

        @pl.when(used > 0.5)
        def _():
            sweep(i, False)

        return carry

    lax.fori_loop(0, n_below, below, 0)
    sweep(n_below, True)
    o_s = acc_s[...] * (1.0 / l_s[...])

    kw = jnp.concatenate([r[...] for r in kw_refs], axis=0)
    vw = jnp.concatenate([r[...] for r in vw_refs], axis=0)
    sw = lax.dot_general(qs, kw, _NT, preferred_element_type=F32)
    pos_w = t0 - WINDOW + lax.broadcasted_iota(jnp.int32, (1, WINDOW + qn), 1)
    d_wi = tq - pos_w
    d_w = d_wi.astype(F32)
    hide_w = jnp.where((d_wi >= 0) & (d_wi < WINDOW) & (pos_w >= 0), 0.0, NEG_INF)
    p_heads, inv_l = [], []
    for h in range(hpg):
        s = sw[hrows[h]] * scale + (hide_w - slopes[h] * d_w)
        p = jnp.exp(s - jnp.max(s, axis=-1, keepdims=True))
        inv_l.append(1.0 / jnp.sum(p, axis=-1, keepdims=True))
        p_heads.append(p.astype(BF16))
    o_w = jnp.dot(jnp.concatenate(p_heads, axis=0), vw, preferred_element_type=F32) * jnp.concatenate(inv_l, axis=0)

    gate = gate_ref[...]
    for h in range(hpg):
        g_c, g_s, g_w = (gate[:, b * hpg + h:b * hpg + h + 1] for b in range(3))
        o = g_c * o_c[hrows[h]] + g_s * o_s[hrows[h]] + g_w * o_w[hrows[h]]
        o_ref[:, h * dk:(h + 1) * dk] = o.astype(o_ref.dtype)


def nsa_attention_core(q, gate, k_cmp, v_cmp, kvb, n_groups):
    t, d = q.shape
    gq = NSA_HPG * NSA_DK
    n_qb = t // Q_BLOCK
    nwb = WINDOW // Q_BLOCK
    n_cp = k_cmp.shape[1]
    full = lambda j: pl.BlockSpec((t, NSA_DK), lambda g, qb, j=j: (0, j * n_groups + g))

    def win(j, i):
        return pl.BlockSpec((Q_BLOCK, NSA_DK), lambda g, qb, j=j, i=i: (jnp.maximum(qb - nwb + i, 0), j * n_groups + g))

    in_specs = [pl.BlockSpec((Q_BLOCK, gq), lambda g, qb: (qb, g)),
                pl.BlockSpec((Q_BLOCK, LANES), lambda g, qb: (qb, g)),
                pl.BlockSpec((1, n_cp, NSA_DK), lambda g, qb: (g, 0, 0)),
                pl.BlockSpec((1, n_cp, NSA_DK), lambda g, qb: (g, 0, 0)),
                full(2), full(3)]
    in_specs += [win(4, i) for i in range(nwb + 1)] + [win(5, i) for i in range(nwb + 1)]
    rows = NSA_HPG * Q_BLOCK
    return pl.pallas_call(
        functools.partial(_nsa_kernel, n_heads_total=n_groups * NSA_HPG),
        grid=(n_groups, n_qb),
        in_specs=in_specs,
        out_specs=pl.BlockSpec((Q_BLOCK, gq), lambda g, qb: (qb, g)),
        out_shape=jax.ShapeDtypeStruct((t, d), BF16),
        scratch_shapes=[pltpu.VMEM((rows, 1), F32), pltpu.VMEM((rows, 1), F32), pltpu.VMEM((rows, NSA_DK), F32)],
        compiler_params=_cparams(("parallel", "arbitrary")),
        name="nsa_attention",
    )(q, gate, k_cmp, v_cmp, kvb, kvb, *([kvb] * (2 * (nwb + 1))))


def nsa_shared_kv(hb, w_kv, cmp_pos, cmp_w1, cmp_b1, cmp_w2):
    t = hb.shape[0]
    n_groups = w_kv.shape[1] // (6 * NSA_DK)
    kv = matmul(hb, w_kv.astype(BF16))
    n_chunk = t // CMP_STRIDE
    cmp = []
    for j in range(2):
        z = kv[:, j * n_groups * NSA_DK:(j + 1) * n_groups * NSA_DK]
        ch = z.reshape(n_chunk, CMP_STRIDE, n_groups, NSA_DK).transpose(2, 0, 1, 3).reshape(n_groups, n_chunk, CMP_STRIDE * NSA_DK)
        nxt = jnp.concatenate([ch[:, 1:], jnp.zeros_like(ch[:, :1])], axis=1)
        flat = jnp.concatenate([ch, nxt], axis=-1).reshape(n_groups * n_chunk, CMP_BLOCK * NSA_DK)
        hid = matmul(flat, cmp_w1[j].astype(BF16), out_dtype=BF16, act=jax.nn.silu,
                     a_bias=cmp_pos[j].reshape(-1), bias=cmp_b1[j])
        out = matmul(hid, cmp_w2[j].astype(BF16))
        cmp.append(out.reshape(n_groups, n_chunk, NSA_DK).astype(BF16))
    return cmp[0], cmp[1], kv.astype(BF16)


def nsa_layer(hb, shared, w_qg, b_g):
    k_cmp, v_cmp, kvb = shared
    n_groups = k_cmp.shape[0]
    n_heads = n_groups * NSA_HPG
    dq = n_heads * NSA_DK
    q = matmul(hb, w_qg[:, :dq].astype(BF16), out_dtype=BF16)
    pad = LANES - 3 * NSA_HPG
    w_g = w_qg[:, dq:].reshape(-1, n_groups, NSA_HPG, 3).transpose(0, 1, 3, 2).reshape(-1, n_groups, 3 * NSA_HPG)
    w_g = jnp.pad(w_g, ((0, 0), (0, 0), (0, pad))).reshape(-1, n_groups * LANES)
    b_gp = b_g.reshape(n_groups, NSA_HPG, 3).transpose(0, 2, 1).reshape(n_groups, 3 * NSA_HPG)
    b_gp = jnp.pad(b_gp, ((0, 0), (0, pad))).reshape(-1)
    gate = matmul(hb, w_g.astype(BF16), act=jax.nn.sigmoid, bias=b_gp)
    return nsa_attention_core(q, gate, k_cmp, v_cmp, kvb, n_groups)


def kernel(x, p, ln_g, ln_b, ffn1_w13, ffn1_w2, ffn2_w13, ffn2_w2, ple_w, ple_gate_down, ple_gate_up, rw_mix, rw_w_rkv, rw_w_o, rw_w0, rw_w1, rw_w2, rw_a0, rw_a1, rw_a2, rw_g1, rw_g2, rw_k_k, rw_k_a, rw_r_k, rw_lnx_g, rw_lnx_b, rw_v0, rw_v1, rw_v2, nsa_w_kv, cmp_pos, cmp_w1, cmp_b1, cmp_w2, nsa_w_qg, nsa_b_g, nsa_w_o):
    batch, t, d = x.shape
    depth = ln_g.shape[0]
    n_a = rw_mix.shape[0]
    outs = []
    for bi in range(batch):
        h = x[bi]
        hb = h.astype(BF16)
        v_first = None
        shared = None
        for i in range(depth):
            mid = swiglu_up(hb, ffn1_w13[i].astype(BF16))
            h, hb = mm_res_ln(mid, ffn1_w2[i].astype(BF16), h, ln_g[i, 0], ln_b[i, 0], scale=0.5)
            if i < n_a:
                v_lora = None if i == 0 else (rw_v0[i - 1], rw_v1[i - 1], rw_v2[i - 1])
                mix_pre, v_first = rwkv7_time_mix(
                    h, v_first, rw_mix[i], rw_w_rkv[i], rw_w_o[i], rw_w0[i], rw_w1[i], rw_w2[i], rw_a0[i],
                    rw_a1[i], rw_a2[i], rw_g1[i], rw_g2[i], rw_k_k[i], rw_k_a[i], rw_r_k[i], rw_lnx_g[i],
                    rw_lnx_b[i], v_lora)
                w_o = rw_w_o[i]
            else:
                j = i - n_a
                mix_pre = nsa_layer(hb, shared, nsa_w_qg[j], nsa_b_g[j])
                w_o = nsa_w_o[j]
            h, hb = mm_res_ln(mix_pre, w_o.astype(BF16), h, ln_g[i, 1], ln_b[i, 1], scale=1.0)
            mid = swiglu_up(hb, ffn2_w13[i].astype(BF16))
            h, hb = mm_res_ln(mid, ffn2_w2[i].astype(BF16), h, ln_g[i, 2], ln_b[i, 2], scale=0.5)
            tdown = matmul(hb, ple_gate_down[i].astype(BF16), out_dtype=BF16)
            h, hb = ple_ln(tdown, p[i, bi], ple_gate_up[i].astype(BF16), ple_w[i].astype(BF16), h,
                           ln_g[i, 3], ln_b[i, 3])
            if i == n_a - 1:
                shared = nsa_shared_kv(hb, nsa_w_kv, cmp_pos, cmp_w1, cmp_b1, cmp_w2)
        outs.append(h)
    return jnp.stack(outs, axis=0)
```

```python
import functools
import math

import jax
import jax.numpy as jnp
from jax import lax
from jax.experimental import pallas as pl
from jax.experimental.pallas import tpu as pltpu

F32 = jnp.float32
BF16 = jnp.bfloat16

V7X_VMEM_BYTES = 64 * 1024 * 1024
VMEM_LIMIT = V7X_VMEM_BYTES - 8 * 1024 * 1024
LANES = 128

MM_TM_BYTES = 8 * 1024 * 1024
MM_TN = 512

LN_EPS = 1e-5
LN_SLAB = 64
DEPTH_TOTAL = 4
ALPHA = (2.0 * DEPTH_TOTAL) ** 0.25

RW_HEAD = 64
RW_GN_EPS = 64e-5
RW_CHUNK = 64
RW_LANES = 256

NSA_DK = 128
NSA_HPG = 8
CMP_BLOCK = 32
CMP_STRIDE = 16
SEL_BLOCK = 64
SEL_TOP_N = 16
WINDOW = 512
Q_BLOCK = 128
FORCE_SCORE = 1e4
NEG_INF = -1e30
SEL_TILE = 512

_NN = (((1,), (0,)), ((), ()))
_NT = (((1,), (1,)), ((), ()))
_TN = (((0,), (0,)), ((), ()))


def _cparams(sem):
    return pltpu.CompilerParams(dimension_semantics=sem, vmem_limit_bytes=VMEM_LIMIT)


def _pick(n, pref):
    if n <= pref:
        return n
    t = pref
    while n % t:
        t //= 2
    return t


def _mm_kernel(*refs, act, has_abias, has_bias):
    a_ref, b_ref = refs[0], refs[1]
    i = 2
    abias_ref = bias_ref = None
    if has_abias:
        abias_ref = refs[i]; i += 1
    if has_bias:
        bias_ref = refs[i]; i += 1
    o_ref = refs[i]
    a = a_ref[...]
    if has_abias:
        a = a.astype(F32) + abias_ref[...]
    z = jnp.dot(a.astype(BF16), b_ref[...], preferred_element_type=F32)
    if has_bias:
        z = z + bias_ref[...]
    if act is not None:
        z = act(z)
    o_ref[...] = z.astype(o_ref.dtype)


def _whole_k_tiles(m, n, kd, a_itemsize):
    assert kd * MM_TN * 2 * 2 <= VMEM_LIMIT // 4, kd
    return _pick(m, MM_TM_BYTES // (kd * a_itemsize)), _pick(n, MM_TN)


def matmul(a, b, *, lead=None, out_dtype=F32, act=None, a_bias=None, bias=None):
    m, kd = a.shape[-2:]
    _, n = b.shape
    tm, tn = _whole_k_tiles(m, n, kd, a.dtype.itemsize)
    if lead is None:
        a_spec = pl.BlockSpec((tm, kd), lambda i, j: (i, 0))
    else:
        a_spec = pl.BlockSpec((None, tm, kd), lambda i, j: (lead, i, 0))
    in_specs = [a_spec, pl.BlockSpec((kd, tn), lambda i, j: (0, j))]
    args = [a, b]
    if a_bias is not None:
        in_specs.append(pl.BlockSpec((1, kd), lambda i, j: (0, 0)))
        args.append(a_bias.reshape(1, kd).astype(F32))
    if bias is not None:
        in_specs.append(pl.BlockSpec((1, tn), lambda i, j: (0, j)))
        args.append(bias.reshape(1, n).astype(F32))
    return pl.pallas_call(
        functools.partial(_mm_kernel, act=act, has_abias=a_bias is not None, has_bias=bias is not None),
        grid=(m // tm, n // tn),
        in_specs=in_specs,
        out_specs=pl.BlockSpec((tm, tn), lambda i, j: (i, j)),
        out_shape=jax.ShapeDtypeStruct((m, n), out_dtype),
        compiler_params=_cparams(("parallel", "parallel")),
        name="matmul",
    )(*args)


def _swiglu_kernel(a_ref, b1_ref, b3_ref, o_ref):
    a = a_ref[...].astype(BF16)
    g = jnp.dot(a, b1_ref[...], preferred_element_type=F32)
    u = jnp.dot(a, b3_ref[...], preferred_element_type=F32)
    o_ref[...] = (g * jax.nn.sigmoid(g) * u).astype(o_ref.dtype)


def swiglu_up(a, w13):
    m, kd = a.shape
    f = w13.shape[1] // 2
    tm, tn = _whole_k_tiles(m, f, kd, a.dtype.itemsize)
    nj = f // tn
    return pl.pallas_call(
        _swiglu_kernel,
        grid=(m // tm, nj),
        in_specs=[pl.BlockSpec((tm, kd), lambda i, j: (i, 0)),
                  pl.BlockSpec((kd, tn), lambda i, j: (0, j)),
                  pl.BlockSpec((kd, tn), lambda i, j: (0, j + nj))],
        out_specs=pl.BlockSpec((tm, tn), lambda i, j: (i, j)),
        out_shape=jax.ShapeDtypeStruct((m, f), BF16),
        compiler_params=_cparams(("parallel", "parallel")),
        name="swiglu_up",
    )(a, w13, w13)


def _deepnorm(z, g, b):
    mu = jnp.mean(z, axis=-1, keepdims=True)
    zc = z - mu
    var = jnp.mean(zc * zc, axis=-1, keepdims=True)
    return zc * lax.rsqrt(var + LN_EPS) * g + b


def _mm_res_ln_kernel(a_ref, b_ref, h_ref, g_ref, beta_ref, o_ref, ob_ref, z_ref, *, nj, tn, scale):
    j = pl.program_id(1)
    z_ref[j] = ALPHA * h_ref[...] + scale * jnp.dot(a_ref[...].astype(BF16), b_ref[...], preferred_element_type=F32)

    @pl.when(j == nj - 1)
    def _():
        inv_n = 1.0 / (nj * tn)

        def slab(i, carry):
            rows = pl.ds(pl.multiple_of(i * LN_SLAB, LN_SLAB), LN_SLAB)
            zs = [z_ref[jj, rows, :] for jj in range(nj)]
            mu = sum(jnp.sum(z, axis=-1, keepdims=True) for z in zs) * inv_n
            zc = [z - mu for z in zs]
            var = sum(jnp.sum(z * z, axis=-1, keepdims=True) for z in zc) * inv_n
            rstd = lax.rsqrt(var + LN_EPS)
            for jj in range(nj):
                cols = slice(jj * tn, (jj + 1) * tn)
                y = zc[jj] * rstd * g_ref[:, cols] + beta_ref[:, cols]
                o_ref[rows, cols] = y
                ob_ref[rows, cols] = y.astype(BF16)
            return carry

        lax.fori_loop(0, o_ref.shape[0] // LN_SLAB, slab, 0)


def mm_res_ln(a, b, h, g, beta, *, scale, tm=256):
    m, kd = a.shape
    n = b.shape[1]
    tm, tn = _pick(m, tm), _pick(n, MM_TN)
    assert tm % LN_SLAB == 0, (tm, LN_SLAB)
    nj = n // tn
    return pl.pallas_call(
        functools.partial(_mm_res_ln_kernel, nj=nj, tn=tn, scale=scale),
        grid=(m // tm, nj),
        in_specs=[pl.BlockSpec((tm, kd), lambda i, j: (i, 0)),
                  pl.BlockSpec((kd, tn), lambda i, j: (0, j)),
                  pl.BlockSpec((tm, tn), lambda i, j: (i, j)),
                  pl.BlockSpec((1, n), lambda i, j: (0, 0)),
                  pl.BlockSpec((1, n), lambda i, j: (0, 0))],
        out_specs=[pl.BlockSpec((tm, n), lambda i, j: (i, 0)),
                   pl.BlockSpec((tm, n), lambda i, j: (i, 0))],
        out_shape=[jax.ShapeDtypeStruct((m, n), F32), jax.ShapeDtypeStruct((m, n), BF16)],
        scratch_shapes=[pltpu.VMEM((nj, tm, tn), F32)],
        compiler_params=_cparams(("parallel", "arbitrary")),
        name="mm_res_ln",
    )(a, b, h, g.reshape(1, n), beta.reshape(1, n))


def _ple_kernel(t_ref, p_ref, gu_ref, pw_ref, h_ref, g_ref, beta_ref, o_ref, ob_ref):
    gate = jax.nn.sigmoid(jnp.dot(t_ref[...], gu_ref[...], preferred_element_type=F32))
    emb = jnp.dot(p_ref[...].astype(BF16), pw_ref[...], preferred_element_type=F32)
    y = _deepnorm(ALPHA * h_ref[...] + gate * emb, g_ref[...], beta_ref[...])
    o_ref[...] = y
    ob_ref[...] = y.astype(BF16)


def ple_ln(t, p, gu, pw, h, g, beta, *, tm=256):
    m, n = h.shape
    e = t.shape[1]
    tm = _pick(m, tm)
    row = lambda i: (i, 0)
    fixed = lambda i: (0, 0)
    return pl.pallas_call(
        _ple_kernel,
        grid=(m // tm,),
        in_specs=[pl.BlockSpec((tm, e), row), pl.BlockSpec((tm, e), row),
                  pl.BlockSpec((e, n), fixed), pl.BlockSpec((e, n), fixed),
                  pl.BlockSpec((tm, n), row), pl.BlockSpec((1, n), fixed), pl.BlockSpec((1, n), fixed)],
        out_specs=[pl.BlockSpec((tm, n), row), pl.BlockSpec((tm, n), row)],
        out_shape=[jax.ShapeDtypeStruct((m, n), F32), jax.ShapeDtypeStruct((m, n), BF16)],
        compiler_params=_cparams(("parallel",)),
        name="ple_ln",
    )(t, p, gu, pw, h, g.reshape(1, n), beta.reshape(1, n))


def _split2(x):
    hi = x.astype(BF16)
    return hi, (x - hi.astype(F32)).astype(BF16)


def _mm3(a, b, dims=_NN):
    dg = lambda x, y: lax.dot_general(x, y, dims, preferred_element_type=F32)
    return dg(a[0], b[0]) + (dg(a[0], b[1]) + dg(a[1], b[0]))


def _mm1(a, b, dims=_NN):
    return lax.dot_general(a, b, dims, preferred_element_type=F32)


def _split3(x):
    hi = x.astype(BF16)
    r1 = x - hi.astype(F32)
    mid = r1.astype(BF16)
    return hi, mid, (r1 - mid.astype(F32)).astype(BF16)


def _rw_mix_kernel(h_ref, hs_ref, mix_ref, o_ref):
    x = h_ref[...]
    xx = hs_ref[...] - x
    for c in range(6):
        o_ref[c] = (x + xx * mix_ref[c:c + 1, :]).astype(BF16)


def rw_mix(h, h_shift, mix, *, tm=256, tn=1024):
    m, n = h.shape
    tm, tn = _pick(m, tm), _pick(n, tn)
    return pl.pallas_call(
        _rw_mix_kernel,
        grid=(m // tm, n // tn),
        in_specs=[pl.BlockSpec((tm, tn), lambda i, j: (i, j)),
                  pl.BlockSpec((tm, tn), lambda i, j: (i, j)),
                  pl.BlockSpec((8, tn), lambda i, j: (0, j))],
        out_specs=pl.BlockSpec((6, tm, tn), lambda i, j: (0, i, j)),
        out_shape=jax.ShapeDtypeStruct((6, m, n), BF16),
        compiler_params=_cparams(("parallel", "parallel")),
        name="rw_mix",
    )(h, h_shift, jnp.pad(mix, ((0, 2), (0, 0))))


_RP_W0, _RP_A0, _RP_V0, _RP_KK, _RP_KA, _RP_RK, _RP_LNG, _RP_LNB = range(8)


def _each(f, *lists):
    return [f(*xs) for xs in zip(*lists)]


def _rw_chunk_local(a_t, r_t, b_h, k_h, v, p_last, strict, incl, eye):
    sp = lambda xs: _each(_split2, xs)
    rnd = lambda xs: _each(lambda z: z.astype(BF16), xs)
    nt = lambda a, b: _mm3(a, b, _NT)
    tn = lambda a, b: _mm1(a, b, _TN)
    at, rt, bh, kh, vs = sp(a_t), sp(r_t), sp(b_h), sp(k_h), sp(v)
    ath, bhh, khh, vsh = ([z[0] for z in zs] for zs in (at, bh, kh, vs))
    m_ab = _each(lambda z: jnp.where(strict, z, 0.0), _each(nt, at, bh))
    m_ak = _each(lambda z: jnp.where(strict, z, 0.0), _each(nt, at, kh))
    m_rb = _each(lambda z: jnp.where(incl, z, 0.0), _each(nt, rt, bh))
    m_rk = _each(lambda z: jnp.where(incl, z, 0.0), _each(nt, rt, kh))
    x = _each(lambda z: eye + z, m_ab)
    pw = m_ab
    for _ in range(max(1, int(math.ceil(math.log2(a_t[0].shape[0]))) - 1)):
        ps = rnd(pw)
        pw = _each(_mm1, ps, ps)
        x = _each(lambda z, d: z + d, x, _each(_mm1, rnd(x), rnd(pw)))
    xs = rnd(x)
    a_x = _each(_mm1, xs, ath)
    u0 = _each(_mm1, xs, rnd(_each(_mm1, rnd(m_ak), vsh)))
    axs, u0s, rbs = sp(a_x), sp(u0), sp(m_rb)
    r_hat = _each(lambda z, d: z + d, r_t, _each(_mm3, rbs, axs))
    y0 = _each(lambda z, d: z + d, _each(_mm3, rbs, u0s), _each(_mm3, sp(m_rk), vs))
    axh, u0h = [z[0] for z in axs], [z[0] for z in u0s]
    t_m = _each(lambda z, p: (eye + z) * p, _each(tn, axh, bhh), p_last)
    s_add = _each(lambda z, d, p: (z + d) * p, _each(tn, u0h, bhh), _each(tn, vsh, khh), p_last)
    return r_hat, y0, t_m, s_add


def _rw_rec_kernel(*refs, has_vres, n_chunks):
    if has_vres:
        (r_ref, k_ref, v_ref, vf_ref, hw_ref, ha_ref, hg_ref, hv_ref, w2_ref, a2_ref, g2_ref, v2_ref,
         rp_ref, o_ref, s_ref, at_s, rt_s, bh_s, kh_s, v_s, g_s, bon_s, rh_s, y0_s, tm_s, sa_s) = refs
    else:
        (r_ref, k_ref, v_ref, hw_ref, ha_ref, hg_ref, w2_ref, a2_ref, g2_ref,
         rp_ref, o_ref, s_ref, at_s, rt_s, bh_s, kh_s, v_s, g_s, bon_s, rh_s, y0_s, tm_s, sa_s) = refs
    n = RW_HEAD
    c = RW_CHUNK
    heads = RW_LANES // n

    @pl.when(pl.program_id(1) == 0)
    def _():
        s_ref[...] = jnp.zeros_like(s_ref)

    rp = rp_ref[...]
    row = lambda i: rp[i:i + 1, :]
    lane = lax.broadcasted_iota(jnp.int32, (1, RW_LANES), 1)

    def per_head_sum(z):
        out = jnp.zeros_like(z)
        for j in range(heads):
            mine = (lane >= j * n) & (lane < (j + 1) * n)
            out = jnp.where(mine, jnp.sum(jnp.where(mine, z, 0.0), axis=-1, keepdims=True), out)
        return out

    zw = row(_RP_W0) + jnp.dot(hw_ref[...], w2_ref[...], preferred_element_type=F32)
    nz = -zw
    softplus = jnp.maximum(nz, 0.0) + jnp.log1p(jnp.exp(-jnp.abs(nz)))
    lw = -jnp.exp(-softplus - 0.5)
    a = jax.nn.sigmoid(row(_RP_A0) + jnp.dot(ha_ref[...], a2_ref[...], preferred_element_type=F32))
    g_s[...] = jnp.dot(hg_ref[...], g2_ref[...], preferred_element_type=F32)
    r = r_ref[...]
    k = k_ref[...]
    v = v_ref[...]
    if has_vres:
        mixv = jax.nn.sigmoid(row(_RP_V0) + jnp.dot(hv_ref[...], v2_ref[...], preferred_element_type=F32))
        v = v + (vf_ref[...] - v) * mixv
    v_s[...] = v
    kk = k * row(_RP_KK)
    kk = kk * lax.rsqrt(jnp.maximum(per_head_sum(kk * kk), 1e-24))
    k = k * (1.0 + (a - 1.0) * row(_RP_KA))
    bon_s[...] = per_head_sum(r * k * row(_RP_RK)) * v

    ti = lax.broadcasted_iota(jnp.int32, (c, c), 0)
    si = lax.broadcasted_iota(jnp.int32, (c, c), 1)
    strict = si < ti
    incl = si <= ti
    tri_incl = incl.astype(BF16)
    eye = (si == ti).astype(F32)

    p_last = []
    for ci in range(n_chunks):
        rows = slice(ci * c, (ci + 1) * c)
        lw_c = lw[rows]
        cum = sum(jnp.dot(tri_incl, part, preferred_element_type=F32) for part in _split3(lw_c))
        p_in = jnp.exp(cum)
        p_inv = jnp.exp(-cum)
        at_s[rows, :] = -kk[rows] * jnp.exp(cum - lw_c)
        rt_s[rows, :] = r[rows] * p_in
        bh_s[rows, :] = kk[rows] * a[rows] * p_inv
        kh_s[rows, :] = k[rows] * p_inv
        p_last.append(p_in[c - 1:c, :])

    where = [(slice(ci * c, (ci + 1) * c), slice(j * n, (j + 1) * n)) for ci in range(n_chunks) for j in range(heads)]
    r_hat, y0, t_m, s_add = _rw_chunk_local(
        [at_s[w] for w in where], [rt_s[w] for w in where], [bh_s[w] for w in where], [kh_s[w] for w in where],
        [v_s[w] for w in where], [p_last[ci][:, j * n:(j + 1) * n] for ci in range(n_chunks) for j in range(heads)],
        strict, incl, eye)
    for i, w in enumerate(where):
        rh_s[w] = r_hat[i]
        y0_s[w] = y0[i]
        tm_s[i] = t_m[i]
        sa_s[i] = s_add[i]

    s = [s_ref[j] for j in range(heads)]
    for ci in range(n_chunks):
        rows = slice(ci * c, (ci + 1) * c)
        for j in range(heads):
            ln = slice(j * n, (j + 1) * n)
            ss = _split2(s[j])
            y = _mm3(_split2(rh_s[rows, ln]), ss, _NT) + y0_s[rows, ln]
            s[j] = _mm3(ss, _split2(tm_s[ci * heads + j])) + sa_s[ci * heads + j]
            mu = jnp.mean(y, axis=-1, keepdims=True)
            yc = y - mu
            var = jnp.mean(yc * yc, axis=-1, keepdims=True)
            yn = yc * lax.rsqrt(var + RW_GN_EPS) * row(_RP_LNG)[:, ln] + row(_RP_LNB)[:, ln]
            o_ref[rows, ln] = ((yn + bon_s[rows, ln]) * g_s[rows, ln]).astype(o_ref.dtype)
    for j in range(heads):
        s_ref[j] = s[j]


def rw_recurrence(r, k, v, v_first, hw, ha, hg, hv, w2, a2, g2, v2, rowp, *, tb=512):
    t, d = r.shape
    tb = _pick(t, tb)
    has_vres = v_first is not None
    tile = pl.BlockSpec((tb, RW_LANES), lambda hb, i: (i, hb))
    lora = lambda w: pl.BlockSpec((tb, w), lambda hb, i: (i, 0))
    wcol = lambda w: pl.BlockSpec((w, RW_LANES), lambda hb, i: (0, hb))
    lw_, la_, lg_ = hw.shape[1], ha.shape[1], hg.shape[1]
    if has_vres:
        args = [r, k, v, v_first, hw, ha, hg, hv, w2, a2, g2, v2, rowp]
        in_specs = [tile, tile, tile, tile, lora(lw_), lora(la_), lora(lg_), lora(hv.shape[1]),
                    wcol(lw_), wcol(la_), wcol(lg_), wcol(hv.shape[1]), wcol(8)]
    else:
        args = [r, k, v, hw, ha, hg, w2, a2, g2, rowp]
        in_specs = [tile, tile, tile, lora(lw_), lora(la_), lora(lg_), wcol(lw_), wcol(la_), wcol(lg_), wcol(8)]
    heads = RW_LANES // RW_HEAD
    n_chunks = tb // RW_CHUNK
    per_chunk = pltpu.VMEM((n_chunks * heads, RW_HEAD, RW_HEAD), F32)
    return pl.pallas_call(
        functools.partial(_rw_rec_kernel, has_vres=has_vres, n_chunks=n_chunks),
        grid=(d // RW_LANES, t // tb),
        in_specs=in_specs,
        out_specs=tile,
        out_shape=jax.ShapeDtypeStruct((t, d), BF16),
        scratch_shapes=[pltpu.VMEM((heads, RW_HEAD, RW_HEAD), F32)]
        + [pltpu.VMEM((tb, RW_LANES), F32)] * 9 + [per_chunk, per_chunk],
        compiler_params=_cparams(("parallel", "arbitrary")),
        name="rw_recurrence",
    )(*args)


def _pad_cols(w, mult=LANES):
    p = (-w.shape[-1]) % mult
    return jnp.pad(w, ((0, 0), (0, p))) if p else w


def _pad_rows(w, mult=LANES):
    p = (-w.shape[0]) % mult
    return jnp.pad(w, ((0, p), (0, 0))) if p else w


def rwkv7_time_mix(h, v_first, mix, w_rkv, w_o, w0, w1, w2, a0, a1, a2, g1, g2, k_k, k_a, r_k, lnx_g, lnx_b, v_lora):
    t, d = h.shape
    h_shift = jnp.concatenate([jnp.zeros((1, d), h.dtype), h[:-1]], axis=0)
    xm = rw_mix(h, h_shift, mix)
    wb = w_rkv.astype(BF16)
    r = matmul(xm, wb[0], lead=0)
    k = matmul(xm, wb[1], lead=1)
    v = matmul(xm, wb[2], lead=2)
    hw = matmul(xm, _pad_cols(w1).astype(BF16), lead=3, out_dtype=BF16, act=jnp.tanh)
    ha = matmul(xm, _pad_cols(a1).astype(BF16), lead=4, out_dtype=BF16)
    hg = matmul(xm, _pad_cols(g1).astype(BF16), lead=5, out_dtype=BF16, act=jax.nn.sigmoid)
    zero = jnp.zeros((d,), F32)
    rowp = jnp.stack([w0, a0, zero if v_lora is None else v_lora[0], k_k, k_a, r_k.reshape(d), lnx_g, lnx_b])
    if v_lora is None:
        out = rw_recurrence(r, k, v, None, hw, ha, hg, None, _pad_rows(w2).astype(BF16),
                            _pad_rows(a2).astype(BF16), _pad_rows(g2).astype(BF16), None, rowp)
        v_first = v
    else:
        hv = matmul(xm, _pad_cols(v_lora[1]).astype(BF16), lead=2, out_dtype=BF16)
        out = rw_recurrence(r, k, v, v_first, hw, ha, hg, hv, _pad_rows(w2).astype(BF16),
                            _pad_rows(a2).astype(BF16), _pad_rows(g2).astype(BF16),
                            _pad_rows(v_lora[2]).astype(BF16), rowp)
    return out, v_first


def _nsa_kernel(q_ref, gate_ref, kc_ref, vc_ref, ks_ref, vs_ref, *rest, n_heads_total):
    kw_refs = rest[0:5]
    vw_refs = rest[5:10]
    o_ref, m_s, l_s, acc_s = rest[10:14]
    g = pl.program_id(0)
    qb = pl.program_id(1)
    hpg, dk, qn = NSA_HPG, NSA_DK, Q_BLOCK
    scale = dk ** -0.5
    t0 = qb * qn
    hrows = [slice(h * qn, (h + 1) * qn) for h in range(hpg)]

    q2 = q_ref[...]
    qs = jnp.concatenate([q2[:, h * dk:(h + 1) * dk] for h in range(hpg)], axis=0)
    tq = t0 + lax.broadcasted_iota(jnp.int32, (qn, 1), 0)
    slopes = [jnp.exp2(-8.0 * jnp.full((1, 1), g * hpg + h + 1, jnp.int32).astype(F32) / n_heads_total)
              for h in range(hpg)]


    kc = kc_ref[0]
    n_cp = kc.shape[0]
    s_all = lax.dot_general(qs, kc, _NT, preferred_element_type=F32)
    pos_c = lax.broadcasted_iota(jnp.int32, (1, n_cp), 1) * CMP_STRIDE + (CMP_BLOCK - 1)
    d_c = (tq - pos_c).astype(F32)
    hide_c = jnp.where(d_c >= 0.0, 0.0, NEG_INF)
    p_grp = jnp.zeros((qn, n_cp), F32)
    p_heads = []
    for h in range(hpg):
        s = s_all[hrows[h]] * scale + (hide_c - slopes[h] * d_c)
        m = jnp.max(s, axis=-1, keepdims=True)
        p = jnp.exp(s - m)
        l = jnp.sum(p, axis=-1, keepdims=True)
        p = p * jnp.where(m > 0.5 * NEG_INF, 1.0 / l, 0.0)
        p_grp = p_grp + p
        p_heads.append(p.astype(BF16))
    o_c = jnp.dot(jnp.concatenate(p_heads, axis=0), vc_ref[0], preferred_element_type=F32)

    n_sel = (n_cp * CMP_STRIDE) // SEL_BLOCK
    ratio = SEL_BLOCK // CMP_STRIDE
    n_ov = CMP_BLOCK // CMP_STRIDE
    ci = lax.broadcasted_iota(jnp.int32, (n_cp, n_sel), 0)
    ji = lax.broadcasted_iota(jnp.int32, (n_cp, n_sel), 1)
    off = ci - ratio * ji
    cnt = jnp.maximum(jnp.minimum(jnp.minimum(off + n_ov, ratio - off), jnp.minimum(n_ov, ratio)), 0)
    cnt = cnt.astype(F32).astype(BF16)
    imp = sum(jnp.dot(part, cnt, preferred_element_type=F32) for part in _split3(p_grp))
    cur = tq // SEL_BLOCK
    sj = lax.broadcasted_iota(jnp.int32, (qn, n_sel), 1)
    forced = (sj == 0) | (sj == cur) | (sj == cur - 1)
    work = jnp.where(sj > cur, -1.0, jnp.where(forced, FORCE_SCORE, imp))
    sel = jnp.zeros((qn, n_sel), F32)
    sjf = sj.astype(F32)
    for _ in range(min(SEL_TOP_N, n_sel)):
        mx = jnp.max(work, axis=-1, keepdims=True)
        first = jnp.min(jnp.where(work == mx, sjf, float(n_sel)), axis=-1, keepdims=True)
        pick = sjf == first
        sel = jnp.where(pick, 1.0, sel)
        work = jnp.where(pick, -2.0, work)
    sel_b = sel.astype(BF16)

    m_s[...] = jnp.full_like(m_s, NEG_INF)
    l_s[...] = jnp.zeros_like(l_s)
    acc_s[...] = jnp.zeros_like(acc_s)
    n_keys = ks_ref.shape[0]
    tk = min(SEL_TILE, n_keys)
    bpt = tk // SEL_BLOCK

    def sweep(i, diagonal):
        k0 = pl.multiple_of(i * tk, tk)
        kt = ks_ref[pl.ds(k0, tk), :]
        vt = vs_ref[pl.ds(k0, tk), :]
        st = lax.dot_general(qs, kt, _NT, preferred_element_type=F32)
        bi = lax.broadcasted_iota(jnp.int32, (n_sel, tk), 0)
        ki = lax.broadcasted_iota(jnp.int32, (n_sel, tk), 1)
        expand = (bi == i * bpt + ki // SEL_BLOCK).astype(F32).astype(BF16)
        chosen = jnp.dot(sel_b, expand, preferred_element_type=F32)
        d_s = (tq - (k0 + lax.broadcasted_iota(jnp.int32, (1, tk), 1))).astype(F32)
        hide = (chosen - 1.0) * (-NEG_INF)
        if diagonal:
            hide = jnp.where(d_s >= 0.0, hide, NEG_INF)
        p_heads, alphas = [], []
        for h in range(hpg):
            s = st[hrows[h]] * scale + (hide - slopes[h] * d_s)
            m_old = m_s[hrows[h], :]
            m_new = jnp.maximum(m_old, jnp.max(s, axis=-1, keepdims=True))
            alpha = jnp.exp(m_old - m_new)
            p = jnp.exp(s - m_new)
            l_s[hrows[h], :] = alpha * l_s[hrows[h], :] + jnp.sum(p, axis=-1, keepdims=True)
            m_s[hrows[h], :] = m_new
            alphas.append(alpha)
            p_heads.append(p.astype(BF16))
        pv = jnp.dot(jnp.concatenate(p_heads, axis=0), vt, preferred_element_type=F32)
        acc_s[...] = jnp.concatenate(alphas, axis=0) * acc_s[...] + pv

    n_below = t0 // tk
    block_used = jnp.max(sel, axis=0, keepdims=True)
    block_id = lax.broadcasted_iota(jnp.int32, (1, n_sel), 1)

    def below(i, carry):
        in_tile = (block_id >= i * bpt) & (block_id < (i + 1) * bpt)
        used = jnp.max(jnp.where(in_tile, block_used, 0.0))

        @pl.when(used > 0.5)
        def _():
            sweep(i, False)

        return carry

    lax.fori_loop(0, n_below, below, 0)
    sweep(n_below, True)
    o_s = acc_s[...] * (1.0 / l_s[...])

    kw = jnp.concatenate([r[...] for r in kw_refs], axis=0)
    vw = jnp.concatenate([r[...] for r in vw_refs], axis=0)
    sw = lax.dot_general(qs, kw, _NT, preferred_element_type=F32)
    pos_w = t0 - WINDOW + lax.broadcasted_iota(jnp.int32, (1, WINDOW + qn), 1)
    d_wi = tq - pos_w
    d_w = d_wi.astype(F32)
    hide_w = jnp.where((d_wi >= 0) & (d_wi < WINDOW) & (pos_w >= 0), 0.0, NEG_INF)
    p_heads, inv_l = [], []
    for h in range(hpg):
        s = sw[hrows[h]] * scale + (hide_w - slopes[h] * d_w)
        p = jnp.exp(s - jnp.max(s, axis=-1, keepdims=True))
        inv_l.append(1.0 / jnp.sum(p, axis=-1, keepdims=True))
        p_heads.append(p.astype(BF16))
    o_w = jnp.dot(jnp.concatenate(p_heads, axis=0), vw, preferred_element_type=F32) * jnp.concatenate(inv_l, axis=0)

    gate = gate_ref[...]
    for h in range(hpg):
        g_c, g_s, g_w = (gate[:, b * hpg + h:b * hpg + h + 1] for b in range(3))
        o = g_c * o_c[hrows[h]] + g_s * o_s[hrows[h]] + g_w * o_w[hrows[h]]
        o_ref[:, h * dk:(h + 1) * dk] = o.astype(o_ref.dtype)


def nsa_attention_core(q, gate, k_cmp, v_cmp, kvb, n_groups):
    t, d = q.shape
    gq = NSA_HPG * NSA_DK
    n_qb = t // Q_BLOCK
    nwb = WINDOW // Q_BLOCK
    n_cp = k_cmp.shape[1]
    full = lambda j: pl.BlockSpec((t, NSA_DK), lambda g, qb, j=j: (0, j * n_groups + g))

    def win(j, i):
        return pl.BlockSpec((Q_BLOCK, NSA_DK), lambda g, qb, j=j, i=i: (jnp.maximum(qb - nwb + i, 0), j * n_groups + g))

    in_specs = [pl.BlockSpec((Q_BLOCK, gq), lambda g, qb: (qb, g)),
                pl.BlockSpec((Q_BLOCK, LANES), lambda g, qb: (qb, g)),
                pl.BlockSpec((1, n_cp, NSA_DK), lambda g, qb: (g, 0, 0)),
                pl.BlockSpec((1, n_cp, NSA_DK), lambda g, qb: (g, 0, 0)),
                full(2), full(3)]
    in_specs += [win(4, i) for i in range(nwb + 1)] + [win(5, i) for i in range(nwb + 1)]
    rows = NSA_HPG * Q_BLOCK
    return pl.pallas_call(
        functools.partial(_nsa_kernel, n_heads_total=n_groups * NSA_HPG),
        grid=(n_groups, n_qb),
        in_specs=in_specs,
        out_specs=pl.BlockSpec((Q_BLOCK, gq), lambda g, qb: (qb, g)),
        out_shape=jax.ShapeDtypeStruct((t, d), BF16),
        scratch_shapes=[pltpu.VMEM((rows, 1), F32), pltpu.VMEM((rows, 1), F32), pltpu.VMEM((rows, NSA_DK), F32)],
        compiler_params=_cparams(("parallel", "arbitrary")),
        name="nsa_attention",
    )(q, gate, k_cmp, v_cmp, kvb, kvb, *([kvb] * (2 * (nwb + 1))))


def nsa_shared_kv(hb, w_kv, cmp_pos, cmp_w1, cmp_b1, cmp_w2):
    t = hb.shape[0]
    n_groups = w_kv.shape[1] // (6 * NSA_DK)
    kv = matmul(hb, w_kv.astype(BF16))
    n_chunk = t // CMP_STRIDE
    cmp = []
    for j in range(2):
        z = kv[:, j * n_groups * NSA_DK:(j + 1) * n_groups * NSA_DK]
        ch = z.reshape(n_chunk, CMP_STRIDE, n_groups, NSA_DK).transpose(2, 0, 1, 3).reshape(n_groups, n_chunk, CMP_STRIDE * NSA_DK)
        nxt = jnp.concatenate([ch[:, 1:], jnp.zeros_like(ch[:, :1])], axis=1)
        flat = jnp.concatenate([ch, nxt], axis=-1).reshape(n_groups * n_chunk, CMP_BLOCK * NSA_DK)
        hid = matmul(flat, cmp_w1[j].astype(BF16), out_dtype=BF16, act=jax.nn.silu,
                     a_bias=cmp_pos[j].reshape(-1), bias=cmp_b1[j])
        out = matmul(hid, cmp_w2[j].astype(BF16))
        cmp.append(out.reshape(n_groups, n_chunk, NSA_DK).astype(BF16))
    return cmp[0], cmp[1], kv.astype(BF16)


def nsa_layer(hb, shared, w_qg, b_g):
    k_cmp, v_cmp, kvb = shared
    n_groups = k_cmp.shape[0]
    n_heads = n_groups * NSA_HPG
    dq = n_heads * NSA_DK
    q = matmul(hb, w_qg[:, :dq].astype(BF16), out_dtype=BF16)
    pad = LANES - 3 * NSA_HPG
    w_g = w_qg[:, dq:].reshape(-1, n_groups, NSA_HPG, 3).transpose(0, 1, 3, 2).reshape(-1, n_groups, 3 * NSA_HPG)
    w_g = jnp.pad(w_g, ((0, 0), (0, 0), (0, pad))).reshape(-1, n_groups * LANES)
    b_gp = b_g.reshape(n_groups, NSA_HPG, 3).transpose(0, 2, 1).reshape(n_groups, 3 * NSA_HPG)
    b_gp = jnp.pad(b_gp, ((0, 0), (0, pad))).reshape(-1)
    gate = matmul(hb, w_g.astype(BF16), act=jax.nn.sigmoid, bias=b_gp)
    return nsa_attention_core(q, gate, k_cmp, v_cmp, kvb, n_groups)


def kernel(x, p, ln_g, ln_b, ffn1_w13, ffn1_w2, ffn2_w13, ffn2_w2, ple_w, ple_gate_down, ple_gate_up, rw_mix, rw_w_rkv, rw_w_o, rw_w0, rw_w1, rw_w2, rw_a0, rw_a1, rw_a2, rw_g1, rw_g2, rw_k_k, rw_k_a, rw_r_k, rw_lnx_g, rw_lnx_b, rw_v0, rw_v1, rw_v2, nsa_w_kv, cmp_pos, cmp_w1, cmp_b1, cmp_w2, nsa_w_qg, nsa_b_g, nsa_w_o):
    batch, t, d = x.shape
    depth = ln_g.shape[0]
    n_a = rw_mix.shape[0]
    outs = []
    for bi in range(batch):
        h = x[bi]
        hb = h.astype(BF16)
        v_first = None
        shared = None
        for i in range(depth):
            mid = swiglu_up(hb, ffn1_w13[i].astype(BF16))
            h, hb = mm_res_ln(mid, ffn1_w2[i].astype(BF16), h, ln_g[i, 0], ln_b[i, 0], scale=0.5)
            if i < n_a:
                v_lora = None if i == 0 else (rw_v0[i - 1], rw_v1[i - 1], rw_v2[i - 1])
                mix_pre, v_first = rwkv7_time_mix(
                    h, v_first, rw_mix[i], rw_w_rkv[i], rw_w_o[i], rw_w0[i], rw_w1[i], rw_w2[i], rw_a0[i],
                    rw_a1[i], rw_a2[i], rw_g1[i], rw_g2[i], rw_k_k[i], rw_k_a[i], rw_r_k[i], rw_lnx_g[i],
                    rw_lnx_b[i], v_lora)
                w_o = rw_w_o[i]
            else:
                j = i - n_a
                mix_pre = nsa_layer(hb, shared, nsa_w_qg[j], nsa_b_g[j])
                w_o = nsa_w_o[j]
            h, hb = mm_res_ln(mix_pre, w_o.astype(BF16), h, ln_g[i, 1], ln_b[i, 1], scale=1.0)
            mid = swiglu_up(hb, ffn2_w13[i].astype(BF16))
            h, hb = mm_res_ln(mid, ffn2_w2[i].astype(BF16), h, ln_g[i, 2], ln_b[i, 2], scale=0.5)
            tdown = matmul(hb, ple_gate_down[i].astype(BF16), out_dtype=BF16)
            h, hb = ple_ln(tdown, p[i, bi], ple_gate_up[i].astype(BF16), ple_w[i].astype(BF16), h,
                           ln_g[i, 3], ln_b[i, 3])
            if i == n_a - 1:
                shared = nsa_shared_kv(hb, nsa_w_kv, cmp_pos, cmp_w1, cmp_b1, cmp_w2)
        outs.append(h)
    return jnp.stack(outs, axis=0)
```

```python
import functools
import math

import jax
import jax.numpy as jnp
from jax import lax
from jax.experimental import pallas as pl
from jax.experimental.pallas import tpu as pltpu

F32 = jnp.float32
BF16 = jnp.bfloat16

V7X_VMEM_BYTES = 64 * 1024 * 1024
VMEM_LIMIT = V7X_VMEM_BYTES - 8 * 1024 * 1024
LANES = 128

MM_TM_BYTES = 8 * 1024 * 1024
MM_TN = 512

LN_EPS = 1e-5
LN_SLAB = 64
DEPTH_TOTAL = 4
ALPHA = (2.0 * DEPTH_TOTAL) ** 0.25

RW_HEAD = 64
RW_GN_EPS = 64e-5
RW_CHUNK = 64
RW_LANES = 256

NSA_DK = 128
NSA_HPG = 8
CMP_BLOCK = 32
CMP_STRIDE = 16
SEL_BLOCK = 64
SEL_TOP_N = 16
WINDOW = 512
Q_BLOCK = 128
FORCE_SCORE = 1e4
NEG_INF = -1e30
SEL_TILE = 512
CMP_COLS = 256

_NN = (((1,), (0,)), ((), ()))
_NT = (((1,), (1,)), ((), ()))
_TN = (((0,), (0,)), ((), ()))


def _cparams(sem):
    return pltpu.CompilerParams(dimension_semantics=sem, vmem_limit_bytes=VMEM_LIMIT)


def _pick(n, pref):
    if n <= pref:
        return n
    t = pref
    while n % t:
        t //= 2
    return t


def _mm_kernel(*refs, act, has_abias, has_bias):
    a_ref, b_ref = refs[0], refs[1]
    i = 2
    abias_ref = bias_ref = None
    if has_abias:
        abias_ref = refs[i]; i += 1
    if has_bias:
        bias_ref = refs[i]; i += 1
    o_ref = refs[i]
    a = a_ref[...]
    if has_abias:
        a = a.astype(F32) + abias_ref[...]
    z = jnp.dot(a.astype(BF16), b_ref[...], preferred_element_type=F32)
    if has_bias:
        z = z + bias_ref[...]
    if act is not None:
        z = act(z)
    o_ref[...] = z.astype(o_ref.dtype)


def _whole_k_tiles(m, n, kd, a_itemsize):
    assert kd * MM_TN * 2 * 2 <= VMEM_LIMIT // 4, kd
    return _pick(m, MM_TM_BYTES // (kd * a_itemsize)), _pick(n, MM_TN)


def matmul(a, b, *, lead=None, out_dtype=F32, act=None, a_bias=None, bias=None):
    m, kd = a.shape[-2:]
    _, n = b.shape
    tm, tn = _whole_k_tiles(m, n, kd, a.dtype.itemsize)
    if lead is None:
        a_spec = pl.BlockSpec((tm, kd), lambda i, j: (i, 0))
    else:
        a_spec = pl.BlockSpec((None, tm, kd), lambda i, j: (lead, i, 0))
    in_specs = [a_spec, pl.BlockSpec((kd, tn), lambda i, j: (0, j))]
    args = [a, b]
    if a_bias is not None:
        in_specs.append(pl.BlockSpec((1, kd), lambda i, j: (0, 0)))
        args.append(a_bias.reshape(1, kd).astype(F32))
    if bias is not None:
        in_specs.append(pl.BlockSpec((1, tn), lambda i, j: (0, j)))
        args.append(bias.reshape(1, n).astype(F32))
    return pl.pallas_call(
        functools.partial(_mm_kernel, act=act, has_abias=a_bias is not None, has_bias=bias is not None),
        grid=(m // tm, n // tn),
        in_specs=in_specs,
        out_specs=pl.BlockSpec((tm, tn), lambda i, j: (i, j)),
        out_shape=jax.ShapeDtypeStruct((m, n), out_dtype),
        compiler_params=_cparams(("parallel", "parallel")),
        name="matmul",
    )(*args)


def _swiglu_kernel(a_ref, b1_ref, b3_ref, o_ref):
    a = a_ref[...].astype(BF16)
    g = jnp.dot(a, b1_ref[...], preferred_element_type=F32)
    u = jnp.dot(a, b3_ref[...], preferred_element_type=F32)
    o_ref[...] = (g * jax.nn.sigmoid(g) * u).astype(o_ref.dtype)


def swiglu_up(a, w13):
    m, kd = a.shape
    f = w13.shape[1] // 2
    tm, tn = _whole_k_tiles(m, f, kd, a.dtype.itemsize)
    nj = f // tn
    return pl.pallas_call(
        _swiglu_kernel,
        grid=(m // tm, nj),
        in_specs=[pl.BlockSpec((tm, kd), lambda i, j: (i, 0)),
                  pl.BlockSpec((kd, tn), lambda i, j: (0, j)),
                  pl.BlockSpec((kd, tn), lambda i, j: (0, j + nj))],
        out_specs=pl.BlockSpec((tm, tn), lambda i, j: (i, j)),
        out_shape=jax.ShapeDtypeStruct((m, f), BF16),
        compiler_params=_cparams(("parallel", "parallel")),
        name="swiglu_up",
    )(a, w13, w13)


def _deepnorm(z, g, b):
    mu = jnp.mean(z, axis=-1, keepdims=True)
    zc = z - mu
    var = jnp.mean(zc * zc, axis=-1, keepdims=True)
    return zc * lax.rsqrt(var + LN_EPS) * g + b


def _mm_res_ln_kernel(a_ref, b_ref, h_ref, g_ref, beta_ref, o_ref, ob_ref, *, nj, tn, scale):
    j = pl.program_id(1)
    cols = pl.ds(pl.multiple_of(j * tn, tn), tn)
    o_ref[:, cols] = ALPHA * h_ref[...] + scale * jnp.dot(a_ref[...].astype(BF16), b_ref[...],
                                                           preferred_element_type=F32)

    @pl.when(j == nj - 1)
    def _():
        def slab(i, carry):
            rows = pl.ds(pl.multiple_of(i * LN_SLAB, LN_SLAB), LN_SLAB)
            y = _deepnorm(o_ref[rows, :], g_ref[...], beta_ref[...])
            o_ref[rows, :] = y
            ob_ref[rows, :] = y.astype(BF16)
            return carry

        lax.fori_loop(0, o_ref.shape[0] // LN_SLAB, slab, 0)


def mm_res_ln(a, b, h, g, beta, *, scale, tm=512):
    m, kd = a.shape
    n = b.shape[1]
    tm, tn = _pick(m, tm), _pick(n, MM_TN)
    assert tm % LN_SLAB == 0, (tm, LN_SLAB)
    nj = n // tn
    return pl.pallas_call(
        functools.partial(_mm_res_ln_kernel, nj=nj, tn=tn, scale=scale),
        grid=(m // tm, nj),
        in_specs=[pl.BlockSpec((tm, kd), lambda i, j: (i, 0)),
                  pl.BlockSpec((kd, tn), lambda i, j: (0, j)),
                  pl.BlockSpec((tm, tn), lambda i, j: (i, j)),
                  pl.BlockSpec((1, n), lambda i, j: (0, 0)),
                  pl.BlockSpec((1, n), lambda i, j: (0, 0))],
        out_specs=[pl.BlockSpec((tm, n), lambda i, j: (i, 0)),
                   pl.BlockSpec((tm, n), lambda i, j: (i, 0))],
        out_shape=[jax.ShapeDtypeStruct((m, n), F32), jax.ShapeDtypeStruct((m, n), BF16)],
        compiler_params=_cparams(("parallel", "arbitrary")),
        name="mm_res_ln",
    )(a, b, h, g.reshape(1, n), beta.reshape(1, n))


def _ple_kernel(t_ref, p_ref, gu_ref, pw_ref, h_ref, g_ref, beta_ref, o_ref, ob_ref):
    gate = jax.nn.sigmoid(jnp.dot(t_ref[...], gu_ref[...], preferred_element_type=F32))
    emb = jnp.dot(p_ref[...].astype(BF16), pw_ref[...], preferred_element_type=F32)
    y = _deepnorm(ALPHA * h_ref[...] + gate * emb, g_ref[...], beta_ref[...])
    o_ref[...] = y
    ob_ref[...] = y.astype(BF16)


def ple_ln(t, p, gu, pw, h, g, beta, *, tm=256):
    m, n = h.shape
    e = t.shape[1]
    tm = _pick(m, tm)
    row = lambda i: (i, 0)
    fixed = lambda i: (0, 0)
    return pl.pallas_call(
        _ple_kernel,
        grid=(m // tm,),
        in_specs=[pl.BlockSpec((tm, e), row), pl.BlockSpec((tm, e), row),
                  pl.BlockSpec((e, n), fixed), pl.BlockSpec((e, n), fixed),
                  pl.BlockSpec((tm, n), row), pl.BlockSpec((1, n), fixed), pl.BlockSpec((1, n), fixed)],
        out_specs=[pl.BlockSpec((tm, n), row), pl.BlockSpec((tm, n), row)],
        out_shape=[jax.ShapeDtypeStruct((m, n), F32), jax.ShapeDtypeStruct((m, n), BF16)],
        compiler_params=_cparams(("parallel",)),
        name="ple_ln",
    )(t, p, gu, pw, h, g.reshape(1, n), beta.reshape(1, n))


def _split2(x):
    hi = x.astype(BF16)
    return hi, (x - hi.astype(F32)).astype(BF16)


def _mm3(a, b, dims=_NN):
    dg = lambda x, y: lax.dot_general(x, y, dims, preferred_element_type=F32)
    return dg(a[0], b[0]) + (dg(a[0], b[1]) + dg(a[1], b[0]))


def _mm1(a, b, dims=_NN):
    return lax.dot_general(a, b, dims, preferred_element_type=F32)


def _split3(x):
    hi = x.astype(BF16)
    r1 = x - hi.astype(F32)
    mid = r1.astype(BF16)
    return hi, mid, (r1 - mid.astype(F32)).astype(BF16)


def _rw_mix_kernel(h_ref, hs_ref, mix_ref, o_ref):
    x = h_ref[...]
    xx = hs_ref[...] - x
    for c in range(6):
        o_ref[c] = (x + xx * mix_ref[c:c + 1, :]).astype(BF16)


def rw_mix(h, h_shift, mix, *, tm=256, tn=1024):
    m, n = h.shape
    tm, tn = _pick(m, tm), _pick(n, tn)
    return pl.pallas_call(
        _rw_mix_kernel,
        grid=(m // tm, n // tn),
        in_specs=[pl.BlockSpec((tm, tn), lambda i, j: (i, j)),
                  pl.BlockSpec((tm, tn), lambda i, j: (i, j)),
                  pl.BlockSpec((8, tn), lambda i, j: (0, j))],
        out_specs=pl.BlockSpec((6, tm, tn), lambda i, j: (0, i, j)),
        out_shape=jax.ShapeDtypeStruct((6, m, n), BF16),
        compiler_params=_cparams(("parallel", "parallel")),
        name="rw_mix",
    )(h, h_shift, jnp.pad(mix, ((0, 2), (0, 0))))


_RP_W0, _RP_A0, _RP_V0, _RP_KK, _RP_KA, _RP_RK, _RP_LNG, _RP_LNB = range(8)


def _each(f, *lists):
    return [f(*xs) for xs in zip(*lists)]


def _rw_chunk_local(a_t, r_t, b_h, k_h, v, p_last, strict, incl, eye):
    sp = lambda xs: _each(_split2, xs)
    rnd = lambda xs: _each(lambda z: z.astype(BF16), xs)
    nt = lambda a, b: _mm3(a, b, _NT)
    tn = lambda a, b: _mm1(a, b, _TN)
    at, rt, bh, kh, vs = sp(a_t), sp(r_t), sp(b_h), sp(k_h), sp(v)
    ath, bhh, khh, vsh = ([z[0] for z in zs] for zs in (at, bh, kh, vs))
    m_ab = _each(lambda z: jnp.where(strict, z, 0.0), _each(nt, at, bh))
    m_ak = _each(lambda z: jnp.where(strict, z, 0.0), _each(nt, at, kh))
    m_rb = _each(lambda z: jnp.where(incl, z, 0.0), _each(nt, rt, bh))
    m_rk = _each(lambda z: jnp.where(incl, z, 0.0), _each(nt, rt, kh))
    x = _each(lambda z: eye + z, m_ab)
    pw = m_ab
    for _ in range(max(1, int(math.ceil(math.log2(a_t[0].shape[0]))) - 1)):
        ps = rnd(pw)
        pw = _each(_mm1, ps, ps)
        x = _each(lambda z, d: z + d, x, _each(_mm1, rnd(x), rnd(pw)))
    xs = rnd(x)
    a_x = _each(_mm1, xs, ath)
    u0 = _each(_mm1, xs, rnd(_each(_mm1, rnd(m_ak), vsh)))
    axs, u0s, rbs = sp(a_x), sp(u0), sp(m_rb)
    r_hat = _each(lambda z, d: z + d, r_t, _each(_mm3, rbs, axs))
    y0 = _each(lambda z, d: z + d, _each(_mm3, rbs, u0s), _each(_mm3, sp(m_rk), vs))
    axh, u0h = [z[0] for z in axs], [z[0] for z in u0s]
    t_m = _each(lambda z, p: (eye + z) * p, _each(tn, axh, bhh), p_last)
    s_add = _each(lambda z, d, p: (z + d) * p, _each(tn, u0h, bhh), _each(tn, vsh, khh), p_last)
    return r_hat, y0, t_m, s_add


def _rw_rec_kernel(*refs, has_vres, n_chunks):
    if has_vres:
        (r_ref, k_ref, v_ref, vf_ref, hw_ref, ha_ref, hg_ref, hv_ref, w2_ref, a2_ref, g2_ref, v2_ref,
         rp_ref, o_ref, s_ref, at_s, rt_s, bh_s, kh_s, v_s, g_s, bon_s, rh_s, y0_s, tm_s, sa_s) = refs
    else:
        (r_ref, k_ref, v_ref, hw_ref, ha_ref, hg_ref, w2_ref, a2_ref, g2_ref,
         rp_ref, o_ref, s_ref, at_s, rt_s, bh_s, kh_s, v_s, g_s, bon_s, rh_s, y0_s, tm_s, sa_s) = refs
    n = RW_HEAD
    c = RW_CHUNK
    heads = RW_LANES // n

    @pl.when(pl.program_id(1) == 0)
    def _():
        s_ref[...] = jnp.zeros_like(s_ref)

    rp = rp_ref[...]
    row = lambda i: rp[i:i + 1, :]
    lane = lax.broadcasted_iota(jnp.int32, (1, RW_LANES), 1)

    def per_head_sum(z):
        out = jnp.zeros_like(z)
        for j in range(heads):
            mine = (lane >= j * n) & (lane < (j + 1) * n)
            out = jnp.where(mine, jnp.sum(jnp.where(mine, z, 0.0), axis=-1, keepdims=True), out)
        return out

    zw = row(_RP_W0) + jnp.dot(hw_ref[...], w2_ref[...], preferred_element_type=F32)
    nz = -zw
    softplus = jnp.maximum(nz, 0.0) + jnp.log1p(jnp.exp(-jnp.abs(nz)))
    lw = -jnp.exp(-softplus - 0.5)
    a = jax.nn.sigmoid(row(_RP_A0) + jnp.dot(ha_ref[...], a2_ref[...], preferred_element_type=F32))
    g_s[...] = jnp.dot(hg_ref[...], g2_ref[...], preferred_element_type=F32)
    r = r_ref[...]
    k = k_ref[...]
    v = v_ref[...]
    if has_vres:
        mixv = jax.nn.sigmoid(row(_RP_V0) + jnp.dot(hv_ref[...], v2_ref[...], preferred_element_type=F32))
        v = v + (vf_ref[...] - v) * mixv
    v_s[...] = v
    kk = k * row(_RP_KK)
    kk = kk * lax.rsqrt(jnp.maximum(per_head_sum(kk * kk), 1e-24))
    k = k * (1.0 + (a - 1.0) * row(_RP_KA))
    bon_s[...] = per_head_sum(r * k * row(_RP_RK)) * v

    ti = lax.broadcasted_iota(jnp.int32, (c, c), 0)
    si = lax.broadcasted_iota(jnp.int32, (c, c), 1)
    strict = si < ti
    incl = si <= ti
    tri_incl = incl.astype(BF16)
    eye = (si == ti).astype(F32)

    p_last = []
    for ci in range(n_chunks):
        rows = slice(ci * c, (ci + 1) * c)
        lw_c = lw[rows]
        cum = sum(jnp.dot(tri_incl, part, preferred_element_type=F32) for part in _split3(lw_c))
        p_in = jnp.exp(cum)
        p_inv = jnp.exp(-cum)
        at_s[rows, :] = -kk[rows] * jnp.exp(cum - lw_c)
        rt_s[rows, :] = r[rows] * p_in
        bh_s[rows, :] = kk[rows] * a[rows] * p_inv
        kh_s[rows, :] = k[rows] * p_inv
        p_last.append(p_in[c - 1:c, :])

    where = [(slice(ci * c, (ci + 1) * c), slice(j * n, (j + 1) * n)) for ci in range(n_chunks) for j in range(heads)]
    r_hat, y0, t_m, s_add = _rw_chunk_local(
        [at_s[w] for w in where], [rt_s[w] for w in where], [bh_s[w] for w in where], [kh_s[w] for w in where],
        [v_s[w] for w in where], [p_last[ci][:, j * n:(j + 1) * n] for ci in range(n_chunks) for j in range(heads)],
        strict, incl, eye)
    for i, w in enumerate(where):
        rh_s[w] = r_hat[i]
        y0_s[w] = y0[i]
        tm_s[i] = t_m[i]
        sa_s[i] = s_add[i]

    s = [s_ref[j] for j in range(heads)]
    for ci in range(n_chunks):
        rows = slice(ci * c, (ci + 1) * c)
        for j in range(heads):
            ln = slice(j * n, (j + 1) * n)
            ss = _split2(s[j])
            y = _mm3(_split2(rh_s[rows, ln]), ss, _NT) + y0_s[rows, ln]
            s[j] = _mm3(ss, _split2(tm_s[ci * heads + j])) + sa_s[ci * heads + j]
            mu = jnp.mean(y, axis=-1, keepdims=True)
            yc = y - mu
            var = jnp.mean(yc * yc, axis=-1, keepdims=True)
            yn = yc * lax.rsqrt(var + RW_GN_EPS) * row(_RP_LNG)[:, ln] + row(_RP_LNB)[:, ln]
            o_ref[rows, ln] = ((yn + bon_s[rows, ln]) * g_s[rows, ln]).astype(o_ref.dtype)
    for j in range(heads):
        s_ref[j] = s[j]


def rw_recurrence(r, k, v, v_first, hw, ha, hg, hv, w2, a2, g2, v2, rowp, *, tb=512):
    t, d = r.shape
    tb = _pick(t, tb)
    has_vres = v_first is not None
    tile = pl.BlockSpec((tb, RW_LANES), lambda hb, i: (i, hb))
    lora = lambda w: pl.BlockSpec((tb, w), lambda hb, i: (i, 0))
    wcol = lambda w: pl.BlockSpec((w, RW_LANES), lambda hb, i: (0, hb))
    lw_, la_, lg_ = hw.shape[1], ha.shape[1], hg.shape[1]
    if has_vres:
        args = [r, k, v, v_first, hw, ha, hg, hv, w2, a2, g2, v2, rowp]
        in_specs = [tile, tile, tile, tile, lora(lw_), lora(la_), lora(lg_), lora(hv.shape[1]),
                    wcol(lw_), wcol(la_), wcol(lg_), wcol(hv.shape[1]), wcol(8)]
    else:
        args = [r, k, v, hw, ha, hg, w2, a2, g2, rowp]
        in_specs = [tile, tile, tile, lora(lw_), lora(la_), lora(lg_), wcol(lw_), wcol(la_), wcol(lg_), wcol(8)]
    heads = RW_LANES // RW_HEAD
    n_chunks = tb // RW_CHUNK
    per_chunk = pltpu.VMEM((n_chunks * heads, RW_HEAD, RW_HEAD), F32)
    return pl.pallas_call(
        functools.partial(_rw_rec_kernel, has_vres=has_vres, n_chunks=n_chunks),
        grid=(d // RW_LANES, t // tb),
        in_specs=in_specs,
        out_specs=tile,
        out_shape=jax.ShapeDtypeStruct((t, d), BF16),
        scratch_shapes=[pltpu.VMEM((heads, RW_HEAD, RW_HEAD), F32)]
        + [pltpu.VMEM((tb, RW_LANES), F32)] * 9 + [per_chunk, per_chunk],
        compiler_params=_cparams(("parallel", "arbitrary")),
        name="rw_recurrence",
    )(*args)


def _pad_cols(w, mult=LANES):
    p = (-w.shape[-1]) % mult
    return jnp.pad(w, ((0, 0), (0, p))) if p else w


def _pad_rows(w, mult=LANES):
    p = (-w.shape[0]) % mult
    return jnp.pad(w, ((0, p), (0, 0))) if p else w


def rwkv7_time_mix(h, v_first, mix, w_rkv, w_o, w0, w1, w2, a0, a1, a2, g1, g2, k_k, k_a, r_k, lnx_g, lnx_b, v_lora):
    t, d = h.shape
    h_shift = jnp.concatenate([jnp.zeros((1, d), h.dtype), h[:-1]], axis=0)
    xm = rw_mix(h, h_shift, mix)
    wb = w_rkv.astype(BF16)
    r = matmul(xm, wb[0], lead=0)
    k = matmul(xm, wb[1], lead=1)
    v = matmul(xm, wb[2], lead=2)
    hw = matmul(xm, _pad_cols(w1).astype(BF16), lead=3, out_dtype=BF16, act=jnp.tanh)
    ha = matmul(xm, _pad_cols(a1).astype(BF16), lead=4, out_dtype=BF16)
    hg = matmul(xm, _pad_cols(g1).astype(BF16), lead=5, out_dtype=BF16, act=jax.nn.sigmoid)
    zero = jnp.zeros((d,), F32)
    rowp = jnp.stack([w0, a0, zero if v_lora is None else v_lora[0], k_k, k_a, r_k.reshape(d), lnx_g, lnx_b])
    if v_lora is None:
        out = rw_recurrence(r, k, v, None, hw, ha, hg, None, _pad_rows(w2).astype(BF16),
                            _pad_rows(a2).astype(BF16), _pad_rows(g2).astype(BF16), None, rowp)
        v_first = v
    else:
        hv = matmul(xm, _pad_cols(v_lora[1]).astype(BF16), lead=2, out_dtype=BF16)
        out = rw_recurrence(r, k, v, v_first, hw, ha, hg, hv, _pad_rows(w2).astype(BF16),
                            _pad_rows(a2).astype(BF16), _pad_rows(g2).astype(BF16),
                            _pad_rows(v_lora[2]).astype(BF16), rowp)
    return out, v_first


def _nsa_kernel(q_ref, gate_ref, kc_ref, vc_ref, ks_ref, vs_ref, *rest, n_heads_total):
    kw_refs = rest[0:5]
    vw_refs = rest[5:10]
    o_ref, m_s, l_s, acc_s, oc_s, imp_s = rest[10:16]
    g = pl.program_id(0)
    qb = pl.program_id(1)
    hpg, dk, qn = NSA_HPG, NSA_DK, Q_BLOCK
    t0 = qb * qn
    hrows = [slice(h * qn, (h + 1) * qn) for h in range(hpg)]

    q2 = q_ref[...]
    qs = jnp.concatenate([q2[:, h * dk:(h + 1) * dk] for h in range(hpg)], axis=0)
    tq = t0 + lax.broadcasted_iota(jnp.int32, (qn, 1), 0)
    slopes = [jnp.exp2(-8.0 * jnp.full((1, 1), g * hpg + h + 1, jnp.int32).astype(F32) / n_heads_total)
              for h in range(hpg)]


    n_cp = kc_ref.shape[1]
    n_sel = (n_cp * CMP_STRIDE) // SEL_BLOCK
    ratio = SEL_BLOCK // CMP_STRIDE
    n_ov = CMP_BLOCK // CMP_STRIDE

    def compressed(ncols):
        kc = kc_ref[0, :ncols, :]
        s_all = lax.dot_general(qs, kc, _NT, preferred_element_type=F32)
        pos_c = lax.broadcasted_iota(jnp.int32, (1, ncols), 1) * CMP_STRIDE + (CMP_BLOCK - 1)
        d_c = (tq - pos_c).astype(F32)
        hide_c = jnp.where(d_c >= 0.0, 0.0, NEG_INF)
        p_grp = jnp.zeros((qn, ncols), F32)
        p_heads = []
        for h in range(hpg):
            s = s_all[hrows[h]] + (hide_c - slopes[h] * d_c)
            m = jnp.max(s, axis=-1, keepdims=True)
            p = jnp.exp(s - m)
            l = jnp.sum(p, axis=-1, keepdims=True)
            p = p * jnp.where(m > 0.5 * NEG_INF, 1.0 / l, 0.0)
            p_grp = p_grp + p
            p_heads.append(p.astype(BF16))
        oc_s[...] = jnp.dot(jnp.concatenate(p_heads, axis=0), vc_ref[0, :ncols, :], preferred_element_type=F32)
        ji = lax.broadcasted_iota(jnp.int32, (n_sel, ncols), 0)
        ci = lax.broadcasted_iota(jnp.int32, (n_sel, ncols), 1)
        off = ci - ratio * ji
        cnt = jnp.maximum(jnp.minimum(jnp.minimum(off + n_ov, ratio - off), jnp.minimum(n_ov, ratio)), 0)
        cnt = cnt.astype(F32).astype(BF16)
        imp_s[...] = sum(lax.dot_general(cnt, part, _NT, preferred_element_type=F32) for part in _split3(p_grp))

    step_c = min(CMP_COLS, n_cp)
    n_var = n_cp // step_c
    need = jnp.minimum(((t0 + qn) // CMP_STRIDE + step_c - 1) // step_c, n_var)
    for b in range(1, n_var + 1):
        @pl.when(need == b)
        def _(b=b):
            compressed(b * step_c)

    tq_l = t0 + lax.broadcasted_iota(jnp.int32, (1, qn), 1)
    cur = tq_l // SEL_BLOCK
    sj = lax.broadcasted_iota(jnp.int32, (n_sel, qn), 0)
    forced = (sj == 0) | (sj == cur) | (sj == cur - 1)
    work = jnp.where(sj > cur, -1.0, jnp.where(forced, FORCE_SCORE, imp_s[...]))
    sel = jnp.zeros((n_sel, qn), F32)
    sjf = sj.astype(F32)
    for _ in range(min(SEL_TOP_N, n_sel)):
        mx = jnp.max(work, axis=0, keepdims=True)
        first = jnp.min(jnp.where(work == mx, sjf, float(n_sel)), axis=0, keepdims=True)
        pick = sjf == first
        sel = jnp.where(pick, 1.0, sel)
        work = jnp.where(pick, -2.0, work)
    sel_b = sel.astype(BF16)

    m_s[...] = jnp.full_like(m_s, NEG_INF)
    l_s[...] = jnp.zeros_like(l_s)
    acc_s[...] = jnp.zeros_like(acc_s)
    n_keys = ks_ref.shape[0]
    tk = min(SEL_TILE, n_keys)
    bpt = tk // SEL_BLOCK

    def sweep(i, diagonal):
        k0 = pl.multiple_of(i * tk, tk)
        kt = ks_ref[pl.ds(k0, tk), :]
        vt = vs_ref[pl.ds(k0, tk), :]
        st = lax.dot_general(qs, kt, _NT, preferred_element_type=F32)
        bi = lax.broadcasted_iota(jnp.int32, (n_sel, tk), 0)
        ki = lax.broadcasted_iota(jnp.int32, (n_sel, tk), 1)
        expand = (bi == i * bpt + ki // SEL_BLOCK).astype(F32).astype(BF16)
        chosen = lax.dot_general(sel_b, expand, _TN, preferred_element_type=F32)
        d_s = (tq - (k0 + lax.broadcasted_iota(jnp.int32, (1, tk), 1))).astype(F32)
        hide = (chosen - 1.0) * (-NEG_INF)
        if diagonal:
            hide = jnp.where(d_s >= 0.0, hide, NEG_INF)
        p_heads, alphas = [], []
        for h in range(hpg):
            s = st[hrows[h]] + (hide - slopes[h] * d_s)
            m_old = m_s[hrows[h], :]
            m_new = jnp.maximum(m_old, jnp.max(s, axis=-1, keepdims=True))
            alpha = jnp.exp(m_old - m_new)
            p = jnp.exp(s - m_new)
            l_s[hrows[h], :] = alpha * l_s[hrows[h], :] + jnp.sum(p, axis=-1, keepdims=True)
            m_s[hrows[h], :] = m_new
            alphas.append(alpha)
            p_heads.append(p.astype(BF16))
        pv = jnp.dot(jnp.concatenate(p_heads, axis=0), vt, preferred_element_type=F32)
        acc_s[...] = jnp.concatenate(alphas, axis=0) * acc_s[...] + pv

    n_below = t0 // tk
    block_used = jnp.max(sel, axis=1, keepdims=True)
    block_id = lax.broadcasted_iota(jnp.int32, (n_sel, 1), 0)

    def below(i, carry):
        in_tile = (block_id >= i * bpt) & (block_id < (i + 1) * bpt)
        used = jnp.max(jnp.where(in_tile, block_used, 0.0))

        @pl.when(used > 0.5)
        def _():
            sweep(i, False)

        return carry

    lax.fori_loop(0, n_below, below, 0)
    sweep(n_below, True)
    o_s = acc_s[...] * (1.0 / l_s[...])

    kw = jnp.concatenate([r[...] for r in kw_refs], axis=0)
    vw = jnp.concatenate([r[...] for r in vw_refs], axis=0)
    sw = lax.dot_general(qs, kw, _NT, preferred_element_type=F32)
    pos_w = t0 - WINDOW + lax.broadcasted_iota(jnp.int32, (1, WINDOW + qn), 1)
    d_wi = tq - pos_w
    d_w = d_wi.astype(F32)
    hide_w = jnp.where((d_wi >= 0) & (d_wi < WINDOW) & (pos_w >= 0), 0.0, NEG_INF)
    p_heads, inv_l = [], []
    for h in range(hpg):
        s = sw[hrows[h]] + (hide_w - slopes[h] * d_w)
        p = jnp.exp(s - jnp.max(s, axis=-1, keepdims=True))
        inv_l.append(1.0 / jnp.sum(p, axis=-1, keepdims=True))
        p_heads.append(p.astype(BF16))
    o_w = jnp.dot(jnp.concatenate(p_heads, axis=0), vw, preferred_element_type=F32) * jnp.concatenate(inv_l, axis=0)

    gate = gate_ref[...]
    o_c = oc_s[...]
    for h in range(hpg):
        g_c, g_s, g_w = (gate[:, b * hpg + h:b * hpg + h + 1] for b in range(3))
        o = g_c * o_c[hrows[h]] + g_s * o_s[hrows[h]] + g_w * o_w[hrows[h]]
        o_ref[:, h * dk:(h + 1) * dk] = o.astype(o_ref.dtype)


def nsa_attention_core(q, gate, k_cmp, v_cmp, kvb, n_groups):
    t, d = q.shape
    gq = NSA_HPG * NSA_DK
    n_qb = t // Q_BLOCK
    nwb = WINDOW // Q_BLOCK
    n_cp = k_cmp.shape[1]
    full = lambda j: pl.BlockSpec((t, NSA_DK), lambda g, qb, j=j: (0, j * n_groups + g))

    def win(j, i):
        return pl.BlockSpec((Q_BLOCK, NSA_DK), lambda g, qb, j=j, i=i: (jnp.maximum(qb - nwb + i, 0), j * n_groups + g))

    in_specs = [pl.BlockSpec((Q_BLOCK, gq), lambda g, qb: (qb, g)),
                pl.BlockSpec((Q_BLOCK, LANES), lambda g, qb: (qb, g)),
                pl.BlockSpec((1, n_cp, NSA_DK), lambda g, qb: (g, 0, 0)),
                pl.BlockSpec((1, n_cp, NSA_DK), lambda g, qb: (g, 0, 0)),
                full(2), full(3)]
    in_specs += [win(4, i) for i in range(nwb + 1)] + [win(5, i) for i in range(nwb + 1)]
    rows = NSA_HPG * Q_BLOCK
    return pl.pallas_call(
        functools.partial(_nsa_kernel, n_heads_total=n_groups * NSA_HPG),
        grid=(n_groups, n_qb),
        in_specs=in_specs,
        out_specs=pl.BlockSpec((Q_BLOCK, gq), lambda g, qb: (qb, g)),
        out_shape=jax.ShapeDtypeStruct((t, d), BF16),
        scratch_shapes=[pltpu.VMEM((rows, 1), F32), pltpu.VMEM((rows, 1), F32), pltpu.VMEM((rows, NSA_DK), F32),
                        pltpu.VMEM((rows, NSA_DK), F32), pltpu.VMEM((t // SEL_BLOCK, Q_BLOCK), F32)],
        compiler_params=_cparams(("parallel", "arbitrary")),
        name="nsa_attention",
    )(q, gate, k_cmp, v_cmp, kvb, kvb, *([kvb] * (2 * (nwb + 1))))


def nsa_shared_kv(hb, w_kv, cmp_pos, cmp_w1, cmp_b1, cmp_w2):
    t = hb.shape[0]
    n_groups = w_kv.shape[1] // (6 * NSA_DK)
    kv = matmul(hb, w_kv.astype(BF16))
    n_chunk = t // CMP_STRIDE
    cmp = []
    for j in range(2):
        z = kv[:, j * n_groups * NSA_DK:(j + 1) * n_groups * NSA_DK]
        ch = z.reshape(n_chunk, CMP_STRIDE, n_groups, NSA_DK).transpose(2, 0, 1, 3).reshape(n_groups, n_chunk, CMP_STRIDE * NSA_DK)
        nxt = jnp.concatenate([ch[:, 1:], jnp.zeros_like(ch[:, :1])], axis=1)
        flat = jnp.concatenate([ch, nxt], axis=-1).reshape(n_groups * n_chunk, CMP_BLOCK * NSA_DK)
        hid = matmul(flat, cmp_w1[j].astype(BF16), out_dtype=BF16, act=jax.nn.silu,
                     a_bias=cmp_pos[j].reshape(-1), bias=cmp_b1[j])
        out = matmul(hid, cmp_w2[j].astype(BF16))
        cmp.append(out.reshape(n_groups, n_chunk, NSA_DK).astype(BF16))
    return cmp[0], cmp[1], kv.astype(BF16)


def nsa_layer(hb, shared, w_qg, b_g):
    k_cmp, v_cmp, kvb = shared
    n_groups = k_cmp.shape[0]
    n_heads = n_groups * NSA_HPG
    dq = n_heads * NSA_DK
    q = matmul(hb, w_qg[:, :dq].astype(BF16), out_dtype=BF16, act=lambda z: z * (NSA_DK ** -0.5))
    pad = LANES - 3 * NSA_HPG
    w_g = w_qg[:, dq:].reshape(-1, n_groups, NSA_HPG, 3).transpose(0, 1, 3, 2).reshape(-1, n_groups, 3 * NSA_HPG)
    w_g = jnp.pad(w_g, ((0, 0), (0, 0), (0, pad))).reshape(-1, n_groups * LANES)
    b_gp = b_g.reshape(n_groups, NSA_HPG, 3).transpose(0, 2, 1).reshape(n_groups, 3 * NSA_HPG)
    b_gp = jnp.pad(b_gp, ((0, 0), (0, pad))).reshape(-1)
    gate = matmul(hb, w_g.astype(BF16), act=jax.nn.sigmoid, bias=b_gp)
    return nsa_attention_core(q, gate, k_cmp, v_cmp, kvb, n_groups)


def kernel(x, p, ln_g, ln_b, ffn1_w13, ffn1_w2, ffn2_w13, ffn2_w2, ple_w, ple_gate_down, ple_gate_up, rw_mix, rw_w_rkv, rw_w_o, rw_w0, rw_w1, rw_w2, rw_a0, rw_a1, rw_a2, rw_g1, rw_g2, rw_k_k, rw_k_a, rw_r_k, rw_lnx_g, rw_lnx_b, rw_v0, rw_v1, rw_v2, nsa_w_kv, cmp_pos, cmp_w1, cmp_b1, cmp_w2, nsa_w_qg, nsa_b_g, nsa_w_o):
    batch, t, d = x.shape
    depth = ln_g.shape[0]
    n_a = rw_mix.shape[0]
    outs = []
    for bi in range(batch):
        h = x[bi]
        hb = h.astype(BF16)
        v_first = None
        shared = None
        for i in range(depth):
            mid = swiglu_up(hb, ffn1_w13[i].astype(BF16))
            h, hb = mm_res_ln(mid, ffn1_w2[i].astype(BF16), h, ln_g[i, 0], ln_b[i, 0], scale=0.5)
            if i < n_a:
                v_lora = None if i == 0 else (rw_v0[i - 1], rw_v1[i - 1], rw_v2[i - 1])
                mix_pre, v_first = rwkv7_time_mix(
                    h, v_first, rw_mix[i], rw_w_rkv[i], rw_w_o[i], rw_w0[i], rw_w1[i], rw_w2[i], rw_a0[i],
                    rw_a1[i], rw_a2[i], rw_g1[i], rw_g2[i], rw_k_k[i], rw_k_a[i], rw_r_k[i], rw_lnx_g[i],
                    rw_lnx_b[i], v_lora)
                w_o = rw_w_o[i]
            else:
                j = i - n_a
                mix_pre = nsa_layer(hb, shared, nsa_w_qg[j], nsa_b_g[j])
                w_o = nsa_w_o[j]
            h, hb = mm_res_ln(mix_pre, w_o.astype(BF16), h, ln_g[i, 1], ln_b[i, 1], scale=1.0)
            mid = swiglu_up(hb, ffn2_w13[i].astype(BF16))
            h, hb = mm_res_ln(mid, ffn2_w2[i].astype(BF16), h, ln_g[i, 2], ln_b[i, 2], scale=0.5)
            tdown = matmul(hb, ple_gate_down[i].astype(BF16), out_dtype=BF16)
            h, hb = ple_ln(tdown, p[i, bi], ple_gate_up[i].astype(BF16), ple_w[i].astype(BF16), h,
                           ln_g[i, 3], ln_b[i, 3])
            if i == n_a - 1:
                shared = nsa_shared_kv(hb, nsa_w_kv, cmp_pos, cmp_w1, cmp_b1, cmp_w2)
        outs.append(h)
    return jnp.stack(outs, axis=0)
```

```python
import functools
import math

import jax
import jax.numpy as jnp
from jax import lax
from jax.experimental import pallas as pl
from jax.experimental.pallas import tpu as pltpu

F32 = jnp.float32
BF16 = jnp.bfloat16

V7X_VMEM_BYTES = 64 * 1024 * 1024
VMEM_LIMIT = V7X_VMEM_BYTES - 8 * 1024 * 1024
LANES = 128

MM_TM_BYTES = 8 * 1024 * 1024
MM_TN = 512

LN_EPS = 1e-5
LN_SLAB = 64
DEPTH_TOTAL = 4
ALPHA = (2.0 * DEPTH_TOTAL) ** 0.25

RW_HEAD = 64
RW_GN_EPS = 64e-5
RW_CHUNK = 64
RW_LANES = 256

NSA_DK = 128
NSA_HPG = 8
CMP_BLOCK = 32
CMP_STRIDE = 16
SEL_BLOCK = 64
SEL_TOP_N = 16
WINDOW = 512
Q_BLOCK = 128
FORCE_SCORE = 1e4
NEG_INF = -1e30
SEL_TILE = 512
CMP_COLS = 256
NSA_SLAB = 16

_NN = (((1,), (0,)), ((), ()))
_NT = (((1,), (1,)), ((), ()))
_TN = (((0,), (0,)), ((), ()))


def _cparams(sem):
    return pltpu.CompilerParams(dimension_semantics=sem, vmem_limit_bytes=VMEM_LIMIT)


def _pick(n, pref):
    if n <= pref:
        return n
    t = pref
    while n % t:
        t //= 2
    return t


def _mm_kernel(*refs, act, has_abias, has_bias):
    a_ref, b_ref = refs[0], refs[1]
    i = 2
    abias_ref = bias_ref = None
    if has_abias:
        abias_ref = refs[i]; i += 1
    if has_bias:
        bias_ref = refs[i]; i += 1
    o_ref = refs[i]
    a = a_ref[...]
    if has_abias:
        a = a.astype(F32) + abias_ref[...]
    z = jnp.dot(a.astype(BF16), b_ref[...], preferred_element_type=F32)
    if has_bias:
        z = z + bias_ref[...]
    if act is not None:
        z = act(z)
    o_ref[...] = z.astype(o_ref.dtype)


def _whole_k_tiles(m, n, kd, a_itemsize):
    assert kd * MM_TN * 2 * 2 <= VMEM_LIMIT // 4, kd
    return _pick(m, MM_TM_BYTES // (kd * a_itemsize)), _pick(n, MM_TN)


def matmul(a, b, *, lead=None, out_dtype=F32, act=None, a_bias=None, bias=None):
    m, kd = a.shape[-2:]
    _, n = b.shape
    tm, tn = _whole_k_tiles(m, n, kd, a.dtype.itemsize)
    if lead is None:
        a_spec = pl.BlockSpec((tm, kd), lambda i, j: (i, 0))
    else:
        a_spec = pl.BlockSpec((None, tm, kd), lambda i, j: (lead, i, 0))
    in_specs = [a_spec, pl.BlockSpec((kd, tn), lambda i, j: (0, j))]
    args = [a, b]
    if a_bias is not None:
        in_specs.append(pl.BlockSpec((1, kd), lambda i, j: (0, 0)))
        args.append(a_bias.reshape(1, kd).astype(F32))
    if bias is not None:
        in_specs.append(pl.BlockSpec((1, tn), lambda i, j: (0, j)))
        args.append(bias.reshape(1, n).astype(F32))
    return pl.pallas_call(
        functools.partial(_mm_kernel, act=act, has_abias=a_bias is not None, has_bias=bias is not None),
        grid=(m // tm, n // tn),
        in_specs=in_specs,
        out_specs=pl.BlockSpec((tm, tn), lambda i, j: (i, j)),
        out_shape=jax.ShapeDtypeStruct((m, n), out_dtype),
        compiler_params=_cparams(("parallel", "parallel")),
        name="matmul",
    )(*args)


def _swiglu_kernel(a_ref, b1_ref, b3_ref, o_ref):
    a = a_ref[...].astype(BF16)
    g = jnp.dot(a, b1_ref[...], preferred_element_type=F32)
    u = jnp.dot(a, b3_ref[...], preferred_element_type=F32)
    o_ref[...] = (g * jax.nn.sigmoid(g) * u).astype(o_ref.dtype)


def swiglu_up(a, w13):
    m, kd = a.shape
    f = w13.shape[1] // 2
    tm, tn = _whole_k_tiles(m, f, kd, a.dtype.itemsize)
    nj = f // tn
    return pl.pallas_call(
        _swiglu_kernel,
        grid=(m // tm, nj),
        in_specs=[pl.BlockSpec((tm, kd), lambda i, j: (i, 0)),
                  pl.BlockSpec((kd, tn), lambda i, j: (0, j)),
                  pl.BlockSpec((kd, tn), lambda i, j: (0, j + nj))],
        out_specs=pl.BlockSpec((tm, tn), lambda i, j: (i, j)),
        out_shape=jax.ShapeDtypeStruct((m, f), BF16),
        compiler_params=_cparams(("parallel", "parallel")),
        name="swiglu_up",
    )(a, w13, w13)


def _deepnorm(z, g, b):
    mu = jnp.mean(z, axis=-1, keepdims=True)
    zc = z - mu
    var = jnp.mean(zc * zc, axis=-1, keepdims=True)
    return zc * lax.rsqrt(var + LN_EPS) * g + b


def _mm_res_ln_kernel(a_ref, b_ref, h_ref, g_ref, beta_ref, o_ref, ob_ref, *, nj, tn, scale):
    j = pl.program_id(1)
    cols = pl.ds(pl.multiple_of(j * tn, tn), tn)
    o_ref[:, cols] = ALPHA * h_ref[...] + scale * jnp.dot(a_ref[...].astype(BF16), b_ref[...],
                                                           preferred_element_type=F32)

    @pl.when(j == nj - 1)
    def _():
        def slab(i, carry):
            rows = pl.ds(pl.multiple_of(i * LN_SLAB, LN_SLAB), LN_SLAB)
            y = _deepnorm(o_ref[rows, :], g_ref[...], beta_ref[...])
            o_ref[rows, :] = y
            ob_ref[rows, :] = y.astype(BF16)
            return carry

        lax.fori_loop(0, o_ref.shape[0] // LN_SLAB, slab, 0)


def mm_res_ln(a, b, h, g, beta, *, scale, tm=512):
    m, kd = a.shape
    n = b.shape[1]
    tm, tn = _pick(m, tm), _pick(n, MM_TN)
    assert tm % LN_SLAB == 0, (tm, LN_SLAB)
    nj = n // tn
    return pl.pallas_call(
        functools.partial(_mm_res_ln_kernel, nj=nj, tn=tn, scale=scale),
        grid=(m // tm, nj),
        in_specs=[pl.BlockSpec((tm, kd), lambda i, j: (i, 0)),
                  pl.BlockSpec((kd, tn), lambda i, j: (0, j)),
                  pl.BlockSpec((tm, tn), lambda i, j: (i, j)),
                  pl.BlockSpec((1, n), lambda i, j: (0, 0)),
                  pl.BlockSpec((1, n), lambda i, j: (0, 0))],
        out_specs=[pl.BlockSpec((tm, n), lambda i, j: (i, 0)),
                   pl.BlockSpec((tm, n), lambda i, j: (i, 0))],
        out_shape=[jax.ShapeDtypeStruct((m, n), F32), jax.ShapeDtypeStruct((m, n), BF16)],
        compiler_params=_cparams(("parallel", "arbitrary")),
        name="mm_res_ln",
    )(a, b, h, g.reshape(1, n), beta.reshape(1, n))


def _ple_kernel(t_ref, p_ref, gu_ref, pw_ref, h_ref, g_ref, beta_ref, o_ref, ob_ref):
    gate = jax.nn.sigmoid(jnp.dot(t_ref[...], gu_ref[...], preferred_element_type=F32))
    emb = jnp.dot(p_ref[...].astype(BF16), pw_ref[...], preferred_element_type=F32)
    y = _deepnorm(ALPHA * h_ref[...] + gate * emb, g_ref[...], beta_ref[...])
    o_ref[...] = y
    ob_ref[...] = y.astype(BF16)


def ple_ln(t, p, gu, pw, h, g, beta, *, tm=256):
    m, n = h.shape
    e = t.shape[1]
    tm = _pick(m, tm)
    row = lambda i: (i, 0)
    fixed = lambda i: (0, 0)
    return pl.pallas_call(
        _ple_kernel,
        grid=(m // tm,),
        in_specs=[pl.BlockSpec((tm, e), row), pl.BlockSpec((tm, e), row),
                  pl.BlockSpec((e, n), fixed), pl.BlockSpec((e, n), fixed),
                  pl.BlockSpec((tm, n), row), pl.BlockSpec((1, n), fixed), pl.BlockSpec((1, n), fixed)],
        out_specs=[pl.BlockSpec((tm, n), row), pl.BlockSpec((tm, n), row)],
        out_shape=[jax.ShapeDtypeStruct((m, n), F32), jax.ShapeDtypeStruct((m, n), BF16)],
        compiler_params=_cparams(("parallel",)),
        name="ple_ln",
    )(t, p, gu, pw, h, g.reshape(1, n), beta.reshape(1, n))


def _split2(x):
    hi = x.astype(BF16)
    return hi, (x - hi.astype(F32)).astype(BF16)


def _mm3(a, b, dims=_NN):
    dg = lambda x, y: lax.dot_general(x, y, dims, preferred_element_type=F32)
    return dg(a[0], b[0]) + (dg(a[0], b[1]) + dg(a[1], b[0]))


def _mm1(a, b, dims=_NN):
    return lax.dot_general(a, b, dims, preferred_element_type=F32)


def _split3(x):
    hi = x.astype(BF16)
    r1 = x - hi.astype(F32)
    mid = r1.astype(BF16)
    return hi, mid, (r1 - mid.astype(F32)).astype(BF16)


def _rw_mix_kernel(h_ref, hs_ref, mix_ref, o_ref):
    x = h_ref[...]
    xx = hs_ref[...] - x
    for c in range(6):
        o_ref[c] = (x + xx * mix_ref[c:c + 1, :]).astype(BF16)


def rw_mix(h, h_shift, mix, *, tm=256, tn=1024):
    m, n = h.shape
    tm, tn = _pick(m, tm), _pick(n, tn)
    return pl.pallas_call(
        _rw_mix_kernel,
        grid=(m // tm, n // tn),
        in_specs=[pl.BlockSpec((tm, tn), lambda i, j: (i, j)),
                  pl.BlockSpec((tm, tn), lambda i, j: (i, j)),
                  pl.BlockSpec((8, tn), lambda i, j: (0, j))],
        out_specs=pl.BlockSpec((6, tm, tn), lambda i, j: (0, i, j)),
        out_shape=jax.ShapeDtypeStruct((6, m, n), BF16),
        compiler_params=_cparams(("parallel", "parallel")),
        name="rw_mix",
    )(h, h_shift, jnp.pad(mix, ((0, 2), (0, 0))))


_RP_W0, _RP_A0, _RP_V0, _RP_KK, _RP_KA, _RP_RK, _RP_LNG, _RP_LNB = range(8)


def _each(f, *lists):
    return [f(*xs) for xs in zip(*lists)]


def _rw_chunk_local(a_t, r_t, b_h, k_h, v, p_last, strict, incl, eye):
    rnd = lambda xs: _each(lambda z: z.astype(BF16), xs)
    nt = lambda a, b: _mm1(a, b, _NT)
    tn = lambda a, b: _mm1(a, b, _TN)
    at, rt, bh, kh, vs = rnd(a_t), rnd(r_t), rnd(b_h), rnd(k_h), rnd(v)
    m_ab = _each(lambda z: jnp.where(strict, z, 0.0), _each(nt, at, bh))
    m_ak = _each(lambda z: jnp.where(strict, z, 0.0), _each(nt, at, kh))
    m_rb = _each(lambda z: jnp.where(incl, z, 0.0), _each(nt, rt, bh))
    m_rk = _each(lambda z: jnp.where(incl, z, 0.0), _each(nt, rt, kh))
    x = _each(lambda z: eye + z, m_ab)
    pw = m_ab
    for _ in range(max(1, int(math.ceil(math.log2(a_t[0].shape[0]))) - 1)):
        ps = rnd(pw)
        pw = _each(_mm1, ps, ps)
        x = _each(lambda z, d: z + d, x, _each(_mm1, rnd(x), rnd(pw)))
    xs = rnd(x)
    ax = rnd(_each(_mm1, xs, at))
    u0 = rnd(_each(_mm1, xs, rnd(_each(_mm1, rnd(m_ak), vs))))
    rb = rnd(m_rb)
    r_hat = _each(lambda z, d: z + d, r_t, _each(_mm1, rb, ax))
    y0 = _each(lambda z, d: z + d, _each(_mm1, rb, u0), _each(_mm1, rnd(m_rk), vs))
    t_m = _each(lambda z, p: (eye + z) * p, _each(tn, ax, bh), p_last)
    s_add = _each(lambda z, d, p: (z + d) * p, _each(tn, u0, bh), _each(tn, vs, kh), p_last)
    return r_hat, y0, t_m, s_add


def _rw_rec_kernel(*refs, has_vres, n_chunks):
    if has_vres:
        (r_ref, k_ref, v_ref, vf_ref, hw_ref, ha_ref, hg_ref, hv_ref, w2_ref, a2_ref, g2_ref, v2_ref,
         rp_ref, o_ref, s_ref, at_s, rt_s, bh_s, kh_s, v_s, g_s, bon_s, rh_s, y0_s, tm_s, sa_s) = refs
    else:
        (r_ref, k_ref, v_ref, hw_ref, ha_ref, hg_ref, w2_ref, a2_ref, g2_ref,
         rp_ref, o_ref, s_ref, at_s, rt_s, bh_s, kh_s, v_s, g_s, bon_s, rh_s, y0_s, tm_s, sa_s) = refs
    n = RW_HEAD
    c = RW_CHUNK
    heads = RW_LANES // n

    @pl.when(pl.program_id(1) == 0)
    def _():
        s_ref[...] = jnp.zeros_like(s_ref)

    rp = rp_ref[...]
    row = lambda i: rp[i:i + 1, :]
    lane = lax.broadcasted_iota(jnp.int32, (1, RW_LANES), 1)

    def per_head_sum(z):
        out = jnp.zeros_like(z)
        for j in range(heads):
            mine = (lane >= j * n) & (lane < (j + 1) * n)
            out = jnp.where(mine, jnp.sum(jnp.where(mine, z, 0.0), axis=-1, keepdims=True), out)
        return out

    zw = row(_RP_W0) + jnp.dot(hw_ref[...], w2_ref[...], preferred_element_type=F32)
    nz = -zw
    softplus = jnp.maximum(nz, 0.0) + jnp.log1p(jnp.exp(-jnp.abs(nz)))
    lw = -jnp.exp(-softplus - 0.5)
    a = jax.nn.sigmoid(row(_RP_A0) + jnp.dot(ha_ref[...], a2_ref[...], preferred_element_type=F32))
    g_s[...] = jnp.dot(hg_ref[...], g2_ref[...], preferred_element_type=F32)
    r = r_ref[...]
    k = k_ref[...]
    v = v_ref[...]
    if has_vres:
        mixv = jax.nn.sigmoid(row(_RP_V0) + jnp.dot(hv_ref[...], v2_ref[...], preferred_element_type=F32))
        v = v + (vf_ref[...] - v) * mixv
    v_s[...] = v
    kk = k * row(_RP_KK)
    kk = kk * lax.rsqrt(jnp.maximum(per_head_sum(kk * kk), 1e-24))
    k = k * (1.0 + (a - 1.0) * row(_RP_KA))
    bon_s[...] = per_head_sum(r * k * row(_RP_RK)) * v

    ti = lax.broadcasted_iota(jnp.int32, (c, c), 0)
    si = lax.broadcasted_iota(jnp.int32, (c, c), 1)
    strict = si < ti
    incl = si <= ti
    tri_incl = incl.astype(BF16)
    eye = (si == ti).astype(F32)

    p_last = []
    for ci in range(n_chunks):
        rows = slice(ci * c, (ci + 1) * c)
        lw_c = lw[rows]
        cum = sum(jnp.dot(tri_incl, part, preferred_element_type=F32) for part in _split3(lw_c))
        p_in = jnp.exp(cum)
        p_inv = jnp.exp(-cum)
        at_s[rows, :] = -kk[rows] * jnp.exp(cum - lw_c)
        rt_s[rows, :] = r[rows] * p_in
        bh_s[rows, :] = kk[rows] * a[rows] * p_inv
        kh_s[rows, :] = k[rows] * p_inv
        p_last.append(p_in[c - 1:c, :])

    where = [(slice(ci * c, (ci + 1) * c), slice(j * n, (j + 1) * n)) for ci in range(n_chunks) for j in range(heads)]
    r_hat, y0, t_m, s_add = _rw_chunk_local(
        [at_s[w] for w in where], [rt_s[w] for w in where], [bh_s[w] for w in where], [kh_s[w] for w in where],
        [v_s[w] for w in where], [p_last[ci][:, j * n:(j + 1) * n] for ci in range(n_chunks) for j in range(heads)],
        strict, incl, eye)
    for i, w in enumerate(where):
        rh_s[w] = r_hat[i]
        y0_s[w] = y0[i]
        tm_s[i] = t_m[i]
        sa_s[i] = s_add[i]

    s = [s_ref[j] for j in range(heads)]
    for ci in range(n_chunks):
        rows = slice(ci * c, (ci + 1) * c)
        for j in range(heads):
            ln = slice(j * n, (j + 1) * n)
            ss = _split2(s[j])
            y = _mm3(_split2(rh_s[rows, ln]), ss, _NT) + y0_s[rows, ln]
            s[j] = _mm3(ss, _split2(tm_s[ci * heads + j])) + sa_s[ci * heads + j]
            mu = jnp.mean(y, axis=-1, keepdims=True)
            yc = y - mu
            var = jnp.mean(yc * yc, axis=-1, keepdims=True)
            yn = yc * lax.rsqrt(var + RW_GN_EPS) * row(_RP_LNG)[:, ln] + row(_RP_LNB)[:, ln]
            o_ref[rows, ln] = ((yn + bon_s[rows, ln]) * g_s[rows, ln]).astype(o_ref.dtype)
    for j in range(heads):
        s_ref[j] = s[j]


def rw_recurrence(r, k, v, v_first, hw, ha, hg, hv, w2, a2, g2, v2, rowp, *, tb=512):
    t, d = r.shape
    tb = _pick(t, tb)
    has_vres = v_first is not None
    tile = pl.BlockSpec((tb, RW_LANES), lambda hb, i: (i, hb))
    lora = lambda w: pl.BlockSpec((tb, w), lambda hb, i: (i, 0))
    wcol = lambda w: pl.BlockSpec((w, RW_LANES), lambda hb, i: (0, hb))
    lw_, la_, lg_ = hw.shape[1], ha.shape[1], hg.shape[1]
    if has_vres:
        args = [r, k, v, v_first, hw, ha, hg, hv, w2, a2, g2, v2, rowp]
        in_specs = [tile, tile, tile, tile, lora(lw_), lora(la_), lora(lg_), lora(hv.shape[1]),
                    wcol(lw_), wcol(la_), wcol(lg_), wcol(hv.shape[1]), wcol(8)]
    else:
        args = [r, k, v, hw, ha, hg, w2, a2, g2, rowp]
        in_specs = [tile, tile, tile, lora(lw_), lora(la_), lora(lg_), wcol(lw_), wcol(la_), wcol(lg_), wcol(8)]
    heads = RW_LANES // RW_HEAD
    n_chunks = tb // RW_CHUNK
    per_chunk = pltpu.VMEM((n_chunks * heads, RW_HEAD, RW_HEAD), F32)
    return pl.pallas_call(
        functools.partial(_rw_rec_kernel, has_vres=has_vres, n_chunks=n_chunks),
        grid=(d // RW_LANES, t // tb),
        in_specs=in_specs,
        out_specs=tile,
        out_shape=jax.ShapeDtypeStruct((t, d), BF16),
        scratch_shapes=[pltpu.VMEM((heads, RW_HEAD, RW_HEAD), F32)]
        + [pltpu.VMEM((tb, RW_LANES), F32)] * 9 + [per_chunk, per_chunk],
        compiler_params=_cparams(("parallel", "arbitrary")),
        name="rw_recurrence",
    )(*args)


def _pad_cols(w, mult=LANES):
    p = (-w.shape[-1]) % mult
    return jnp.pad(w, ((0, 0), (0, p))) if p else w


def _pad_rows(w, mult=LANES):
    p = (-w.shape[0]) % mult
    return jnp.pad(w, ((0, p), (0, 0))) if p else w


def rwkv7_time_mix(h, v_first, mix, w_rkv, w_o, w0, w1, w2, a0, a1, a2, g1, g2, k_k, k_a, r_k, lnx_g, lnx_b, v_lora):
    t, d = h.shape
    h_shift = jnp.concatenate([jnp.zeros((1, d), h.dtype), h[:-1]], axis=0)
    xm = rw_mix(h, h_shift, mix)
    wb = w_rkv.astype(BF16)
    r = matmul(xm, wb[0], lead=0)
    k = matmul(xm, wb[1], lead=1)
    v = matmul(xm, wb[2], lead=2)
    hw = matmul(xm, _pad_cols(w1).astype(BF16), lead=3, out_dtype=BF16, act=jnp.tanh)
    ha = matmul(xm, _pad_cols(a1).astype(BF16), lead=4, out_dtype=BF16)
    hg = matmul(xm, _pad_cols(g1).astype(BF16), lead=5, out_dtype=BF16, act=jax.nn.sigmoid)
    zero = jnp.zeros((d,), F32)
    rowp = jnp.stack([w0, a0, zero if v_lora is None else v_lora[0], k_k, k_a, r_k.reshape(d), lnx_g, lnx_b])
    if v_lora is None:
        out = rw_recurrence(r, k, v, None, hw, ha, hg, None, _pad_rows(w2).astype(BF16),
                            _pad_rows(a2).astype(BF16), _pad_rows(g2).astype(BF16), None, rowp)
        v_first = v
    else:
        hv = matmul(xm, _pad_cols(v_lora[1]).astype(BF16), lead=2, out_dtype=BF16)
        out = rw_recurrence(r, k, v, v_first, hw, ha, hg, hv, _pad_rows(w2).astype(BF16),
                            _pad_rows(a2).astype(BF16), _pad_rows(g2).astype(BF16),
                            _pad_rows(v_lora[2]).astype(BF16), rowp)
    return out, v_first


def _nsa_kernel(q_ref, gate_ref, kc_ref, vc_ref, ks_ref, vs_ref, *rest, n_heads_total):
    kw_refs = rest[0:5]
    vw_refs = rest[5:10]
    o_ref, m_s, l_s, acc_s, oc_s, imp_s, st_s, p_s, pg_s = rest[10:19]
    g = pl.program_id(0)
    qb = pl.program_id(1)
    hpg, dk, qn = NSA_HPG, NSA_DK, Q_BLOCK
    t0 = qb * qn
    hrows = [slice(h * qn, (h + 1) * qn) for h in range(hpg)]

    q2 = q_ref[...]
    qs = jnp.concatenate([q2[:, h * dk:(h + 1) * dk] for h in range(hpg)], axis=0)
    tq = t0 + lax.broadcasted_iota(jnp.int32, (qn, 1), 0)
    slopes = [jnp.exp2(-8.0 * jnp.full((1, 1), g * hpg + h + 1, jnp.int32).astype(F32) / n_heads_total)
              for h in range(hpg)]


    n_cp = kc_ref.shape[1]
    n_sel = (n_cp * CMP_STRIDE) // SEL_BLOCK
    ratio = SEL_BLOCK // CMP_STRIDE
    n_ov = CMP_BLOCK // CMP_STRIDE

    def compressed(ncols):
        kc = kc_ref[0, :ncols, :]
        st_s[:, :ncols] = lax.dot_general(qs, kc, _NT, preferred_element_type=F32)
        pos_c = lax.broadcasted_iota(jnp.int32, (1, ncols), 1) * CMP_STRIDE + (CMP_BLOCK - 1)

        def slab(r, carry):
            r0 = pl.multiple_of(r * NSA_SLAB, NSA_SLAB)
            d_c = (t0 + r0 + lax.broadcasted_iota(jnp.int32, (NSA_SLAB, 1), 0) - pos_c).astype(F32)
            hide_c = jnp.where(d_c >= 0.0, 0.0, NEG_INF)
            rrs = [pl.ds(pl.multiple_of(h * qn + r0, NSA_SLAB), NSA_SLAB) for h in range(hpg)]
            ps = []
            for h in range(hpg):
                s = st_s[rrs[h], :ncols] + (hide_c - slopes[h] * d_c)
                m = jnp.max(s, axis=-1, keepdims=True)
                p = jnp.exp(s - m)
                l = jnp.sum(p, axis=-1, keepdims=True)
                ps.append(p * jnp.where(m > 0.5 * NEG_INF, 1.0 / l, 0.0))
            p_grp = ps[0]
            for h in range(1, hpg):
                p_grp = p_grp + ps[h]
            for h in range(hpg):
                p_s[rrs[h], :ncols] = ps[h].astype(BF16)
            pg_s[pl.ds(r0, NSA_SLAB), :ncols] = p_grp
            return carry

        lax.fori_loop(0, qn // NSA_SLAB, slab, 0)
        p_grp = pg_s[:, :ncols]
        oc_s[...] = jnp.dot(p_s[:, :ncols], vc_ref[0, :ncols, :], preferred_element_type=F32)
        ji = lax.broadcasted_iota(jnp.int32, (n_sel, ncols), 0)
        ci = lax.broadcasted_iota(jnp.int32, (n_sel, ncols), 1)
        off = ci - ratio * ji
        cnt = jnp.maximum(jnp.minimum(jnp.minimum(off + n_ov, ratio - off), jnp.minimum(n_ov, ratio)), 0)
        cnt = cnt.astype(F32).astype(BF16)
        imp_s[...] = sum(lax.dot_general(cnt, part, _NT, preferred_element_type=F32) for part in _split3(p_grp))

    step_c = min(CMP_COLS, n_cp)
    n_var = n_cp // step_c
    need = jnp.minimum(((t0 + qn) // CMP_STRIDE + step_c - 1) // step_c, n_var)
    for b in range(1, n_var + 1):
        @pl.when(need == b)
        def _(b=b):
            compressed(b * step_c)

    tq_l = t0 + lax.broadcasted_iota(jnp.int32, (1, qn), 1)
    cur = tq_l // SEL_BLOCK
    sj = lax.broadcasted_iota(jnp.int32, (n_sel, qn), 0)
    forced = (sj == 0) | (sj == cur) | (sj == cur - 1)
    work = jnp.where(sj > cur, -1.0, jnp.where(forced, FORCE_SCORE, imp_s[...]))
    sel = jnp.zeros((n_sel, qn), F32)
    sjf = sj.astype(F32)
    for _ in range(min(SEL_TOP_N, n_sel)):
        mx = jnp.max(work, axis=0, keepdims=True)
        first = jnp.min(jnp.where(work == mx, sjf, float(n_sel)), axis=0, keepdims=True)
        pick = sjf == first
        sel = jnp.where(pick, 1.0, sel)
        work = jnp.where(pick, -2.0, work)
    sel_b = sel.astype(BF16)

    m_s[...] = jnp.full_like(m_s, NEG_INF)
    l_s[...] = jnp.zeros_like(l_s)
    acc_s[...] = jnp.zeros_like(acc_s)
    n_keys = ks_ref.shape[0]
    tk = min(SEL_TILE, n_keys)
    bpt = tk // SEL_BLOCK

    def sweep(i, diagonal):
        k0 = pl.multiple_of(i * tk, tk)
        kt = ks_ref[pl.ds(k0, tk), :]
        vt = vs_ref[pl.ds(k0, tk), :]
        st = lax.dot_general(qs, kt, _NT, preferred_element_type=F32)
        bi = lax.broadcasted_iota(jnp.int32, (n_sel, tk), 0)
        ki = lax.broadcasted_iota(jnp.int32, (n_sel, tk), 1)
        expand = (bi == i * bpt + ki // SEL_BLOCK).astype(F32).astype(BF16)
        chosen = lax.dot_general(sel_b, expand, _TN, preferred_element_type=F32)
        d_s = (tq - (k0 + lax.broadcasted_iota(jnp.int32, (1, tk), 1))).astype(F32)
        hide = (chosen - 1.0) * (-NEG_INF)
        if diagonal:
            hide = jnp.where(d_s >= 0.0, hide, NEG_INF)
        p_heads, alphas = [], []
        for h in range(hpg):
            s = st[hrows[h]] + (hide - slopes[h] * d_s)
            m_old = m_s[hrows[h], :]
            m_new = jnp.maximum(m_old, jnp.max(s, axis=-1, keepdims=True))
            alpha = jnp.exp(m_old - m_new)
            p = jnp.exp(s - m_new)
            l_s[hrows[h], :] = alpha * l_s[hrows[h], :] + jnp.sum(p, axis=-1, keepdims=True)
            m_s[hrows[h], :] = m_new
            alphas.append(alpha)
            p_heads.append(p.astype(BF16))
        pv = jnp.dot(jnp.concatenate(p_heads, axis=0), vt, preferred_element_type=F32)
        acc_s[...] = jnp.concatenate(alphas, axis=0) * acc_s[...] + pv

    n_below = t0 // tk
    block_used = jnp.max(sel, axis=1, keepdims=True)
    block_id = lax.broadcasted_iota(jnp.int32, (n_sel, 1), 0)

    def below(i, carry):
        in_tile = (block_id >= i * bpt) & (block_id < (i + 1) * bpt)
        used = jnp.max(jnp.where(in_tile, block_used, 0.0))

        @pl.when(used > 0.5)
        def _():
            sweep(i, False)

        return carry

    lax.fori_loop(0, n_below, below, 0)
    sweep(n_below, True)
    o_s = acc_s[...] * (1.0 / l_s[...])

    kw = jnp.concatenate([r[...] for r in kw_refs], axis=0)
    vw = jnp.concatenate([r[...] for r in vw_refs], axis=0)
    sw = lax.dot_general(qs, kw, _NT, preferred_element_type=F32)
    pos_w = t0 - WINDOW + lax.broadcasted_iota(jnp.int32, (1, WINDOW + qn), 1)
    d_wi = tq - pos_w
    d_w = d_wi.astype(F32)
    hide_w = jnp.where((d_wi >= 0) & (d_wi < WINDOW) & (pos_w >= 0), 0.0, NEG_INF)
    p_heads, inv_l = [], []
    for h in range(hpg):
        s = sw[hrows[h]] + (hide_w - slopes[h] * d_w)
        p = jnp.exp(s - jnp.max(s, axis=-1, keepdims=True))
        inv_l.append(1.0 / jnp.sum(p, axis=-1, keepdims=True))
        p_heads.append(p.astype(BF16))
    o_w = jnp.dot(jnp.concatenate(p_heads, axis=0), vw, preferred_element_type=F32) * jnp.concatenate(inv_l, axis=0)

    gate = gate_ref[...]
    o_c = oc_s[...]
    for h in range(hpg):
        g_c, g_s, g_w = (gate[:, b * hpg + h:b * hpg + h + 1] for b in range(3))
        o = g_c * o_c[hrows[h]] + g_s * o_s[hrows[h]] + g_w * o_w[hrows[h]]
        o_ref[:, h * dk:(h + 1) * dk] = o.astype(o_ref.dtype)


def nsa_attention_core(q, gate, k_cmp, v_cmp, kvb, n_groups):
    t, d = q.shape
    gq = NSA_HPG * NSA_DK
    n_qb = t // Q_BLOCK
    nwb = WINDOW // Q_BLOCK
    n_cp = k_cmp.shape[1]
    full = lambda j: pl.BlockSpec((t, NSA_DK), lambda g, qb, j=j: (0, j * n_groups + g))

    def win(j, i):
        return pl.BlockSpec((Q_BLOCK, NSA_DK), lambda g, qb, j=j, i=i: (jnp.maximum(qb - nwb + i, 0), j * n_groups + g))

    in_specs = [pl.BlockSpec((Q_BLOCK, gq), lambda g, qb: (qb, g)),
                pl.BlockSpec((Q_BLOCK, LANES), lambda g, qb: (qb, g)),
                pl.BlockSpec((1, n_cp, NSA_DK), lambda g, qb: (g, 0, 0)),
                pl.BlockSpec((1, n_cp, NSA_DK), lambda g, qb: (g, 0, 0)),
                full(2), full(3)]
    in_specs += [win(4, i) for i in range(nwb + 1)] + [win(5, i) for i in range(nwb + 1)]
    rows = NSA_HPG * Q_BLOCK
    return pl.pallas_call(
        functools.partial(_nsa_kernel, n_heads_total=n_groups * NSA_HPG),
        grid=(n_groups, n_qb),
        in_specs=in_specs,
        out_specs=pl.BlockSpec((Q_BLOCK, gq), lambda g, qb: (qb, g)),
        out_shape=jax.ShapeDtypeStruct((t, d), BF16),
        scratch_shapes=[pltpu.VMEM((rows, 1), F32), pltpu.VMEM((rows, 1), F32),
                        pltpu.VMEM((rows, NSA_DK), F32), pltpu.VMEM((rows, NSA_DK), F32),
                        pltpu.VMEM((t // SEL_BLOCK, Q_BLOCK), F32),
                        pltpu.VMEM((rows, n_cp), F32), pltpu.VMEM((rows, n_cp), BF16),
                        pltpu.VMEM((Q_BLOCK, n_cp), F32)],
        compiler_params=_cparams(("parallel", "arbitrary")),
        name="nsa_attention",
    )(q, gate, k_cmp, v_cmp, kvb, kvb, *([kvb] * (2 * (nwb + 1))))


def nsa_shared_kv(hb, w_kv, cmp_pos, cmp_w1, cmp_b1, cmp_w2):
    t = hb.shape[0]
    n_groups = w_kv.shape[1] // (6 * NSA_DK)
    kv = matmul(hb, w_kv.astype(BF16))
    n_chunk = t // CMP_STRIDE
    cmp = []
    for j in range(2):
        z = kv[:, j * n_groups * NSA_DK:(j + 1) * n_groups * NSA_DK]
        ch = z.reshape(n_chunk, CMP_STRIDE, n_groups, NSA_DK).transpose(2, 0, 1, 3).reshape(n_groups, n_chunk, CMP_STRIDE * NSA_DK)
        nxt = jnp.concatenate([ch[:, 1:], jnp.zeros_like(ch[:, :1])], axis=1)
        flat = jnp.concatenate([ch, nxt], axis=-1).reshape(n_groups * n_chunk, CMP_BLOCK * NSA_DK)
        hid = matmul(flat, cmp_w1[j].astype(BF16), out_dtype=BF16, act=jax.nn.silu,
                     a_bias=cmp_pos[j].reshape(-1), bias=cmp_b1[j])
        out = matmul(hid, cmp_w2[j].astype(BF16))
        cmp.append(out.reshape(n_groups, n_chunk, NSA_DK).astype(BF16))
    return cmp[0], cmp[1], kv.astype(BF16)


def nsa_layer(hb, shared, w_qg, b_g):
    k_cmp, v_cmp, kvb = shared
    n_groups = k_cmp.shape[0]
    n_heads = n_groups * NSA_HPG
    dq = n_heads * NSA_DK
    q = matmul(hb, w_qg[:, :dq].astype(BF16), out_dtype=BF16, act=lambda z: z * (NSA_DK ** -0.5))
    pad = LANES - 3 * NSA_HPG
    w_g = w_qg[:, dq:].reshape(-1, n_groups, NSA_HPG, 3).transpose(0, 1, 3, 2).reshape(-1, n_groups, 3 * NSA_HPG)
    w_g = jnp.pad(w_g, ((0, 0), (0, 0), (0, pad))).reshape(-1, n_groups * LANES)
    b_gp = b_g.reshape(n_groups, NSA_HPG, 3).transpose(0, 2, 1).reshape(n_groups, 3 * NSA_HPG)
    b_gp = jnp.pad(b_gp, ((0, 0), (0, pad))).reshape(-1)
    gate = matmul(hb, w_g.astype(BF16), act=jax.nn.sigmoid, bias=b_gp)
    return nsa_attention_core(q, gate, k_cmp, v_cmp, kvb, n_groups)


def kernel(x, p, ln_g, ln_b, ffn1_w13, ffn1_w2, ffn2_w13, ffn2_w2, ple_w, ple_gate_down, ple_gate_up, rw_mix, rw_w_rkv, rw_w_o, rw_w0, rw_w1, rw_w2, rw_a0, rw_a1, rw_a2, rw_g1, rw_g2, rw_k_k, rw_k_a, rw_r_k, rw_lnx_g, rw_lnx_b, rw_v0, rw_v1, rw_v2, nsa_w_kv, cmp_pos, cmp_w1, cmp_b1, cmp_w2, nsa_w_qg, nsa_b_g, nsa_w_o):
    batch, t, d = x.shape
    depth = ln_g.shape[0]
    n_a = rw_mix.shape[0]
    outs = []
    for bi in range(batch):
        h = x[bi]
        hb = h.astype(BF16)
        v_first = None
        shared = None
        for i in range(depth):
            mid = swiglu_up(hb, ffn1_w13[i].astype(BF16))
            h, hb = mm_res_ln(mid, ffn1_w2[i].astype(BF16), h, ln_g[i, 0], ln_b[i, 0], scale=0.5)
            if i < n_a:
                v_lora = None if i == 0 else (rw_v0[i - 1], rw_v1[i - 1], rw_v2[i - 1])
                mix_pre, v_first = rwkv7_time_mix(
                    h, v_first, rw_mix[i], rw_w_rkv[i], rw_w_o[i], rw_w0[i], rw_w1[i], rw_w2[i], rw_a0[i],
                    rw_a1[i], rw_a2[i], rw_g1[i], rw_g2[i], rw_k_k[i], rw_k_a[i], rw_r_k[i], rw_lnx_g[i],
                    rw_lnx_b[i], v_lora)
                w_o = rw_w_o[i]
            else:
                j = i - n_a
                mix_pre = nsa_layer(hb, shared, nsa_w_qg[j], nsa_b_g[j])
                w_o = nsa_w_o[j]
            h, hb = mm_res_ln(mix_pre, w_o.astype(BF16), h, ln_g[i, 1], ln_b[i, 1], scale=1.0)
            mid = swiglu_up(hb, ffn2_w13[i].astype(BF16))
            h, hb = mm_res_ln(mid, ffn2_w2[i].astype(BF16), h, ln_g[i, 2], ln_b[i, 2], scale=0.5)
            tdown = matmul(hb, ple_gate_down[i].astype(BF16), out_dtype=BF16)
            h, hb = ple_ln(tdown, p[i, bi], ple_gate_up[i].astype(BF16), ple_w[i].astype(BF16), h,
                           ln_g[i, 3], ln_b[i, 3])
            if i == n_a - 1:
                shared = nsa_shared_kv(hb, nsa_w_kv, cmp_pos, cmp_w1, cmp_b1, cmp_w2)
        outs.append(h)
    return jnp.stack(outs, axis=0)
```

```python
import functools
import math

import jax
import jax.numpy as jnp
from jax import lax
from jax.experimental import pallas as pl
from jax.experimental.pallas import tpu as pltpu

F32 = jnp.float32
BF16 = jnp.bfloat16

V7X_VMEM_BYTES = 64 * 1024 * 1024
VMEM_LIMIT = V7X_VMEM_BYTES - 8 * 1024 * 1024
LANES = 128

MM_TM_BYTES = 8 * 1024 * 1024
MM_TN = 512

LN_EPS = 1e-5
LN_SLAB = 64
DEPTH_TOTAL = 4
ALPHA = (2.0 * DEPTH_TOTAL) ** 0.25

RW_HEAD = 64
RW_GN_EPS = 64e-5
RW_CHUNK = 64
RW_LANES = 256

NSA_DK = 128
NSA_HPG = 8
CMP_BLOCK = 32
CMP_STRIDE = 16
SEL_BLOCK = 64
SEL_TOP_N = 16
WINDOW = 512
Q_BLOCK = 128
FORCE_SCORE = 1e4
NEG_INF = -1e30
SEL_TILE = 512
CMP_COLS = 256

_NN = (((1,), (0,)), ((), ()))
_NT = (((1,), (1,)), ((), ()))
_TN = (((0,), (0,)), ((), ()))


def _cparams(sem):
    return pltpu.CompilerParams(dimension_semantics=sem, vmem_limit_bytes=VMEM_LIMIT)


def _pick(n, pref):
    if n <= pref:
        return n
    t = pref
    while n % t:
        t //= 2
    return t


def _mm_kernel(*refs, act, has_abias, has_bias):
    a_ref, b_ref = refs[0], refs[1]
    i = 2
    abias_ref = bias_ref = None
    if has_abias:
        abias_ref = refs[i]; i += 1
    if has_bias:
        bias_ref = refs[i]; i += 1
    o_ref = refs[i]
    a = a_ref[...]
    if has_abias:
        a = a.astype(F32) + abias_ref[...]
    z = jnp.dot(a.astype(BF16), b_ref[...], preferred_element_type=F32)
    if has_bias:
        z = z + bias_ref[...]
    if act is not None:
        z = act(z)
    o_ref[...] = z.astype(o_ref.dtype)


def _whole_k_tiles(m, n, kd, a_itemsize):
    assert kd * MM_TN * 2 * 2 <= VMEM_LIMIT // 4, kd
    return _pick(m, MM_TM_BYTES // (kd * a_itemsize)), _pick(n, MM_TN)


def matmul(a, b, *, lead=None, out_dtype=F32, act=None, a_bias=None, bias=None):
    m, kd = a.shape[-2:]
    _, n = b.shape
    tm, tn = _whole_k_tiles(m, n, kd, a.dtype.itemsize)
    if lead is None:
        a_spec = pl.BlockSpec((tm, kd), lambda i, j: (i, 0))
    else:
        a_spec = pl.BlockSpec((None, tm, kd), lambda i, j: (lead, i, 0))
    in_specs = [a_spec, pl.BlockSpec((kd, tn), lambda i, j: (0, j))]
    args = [a, b]
    if a_bias is not None:
        in_specs.append(pl.BlockSpec((1, kd), lambda i, j: (0, 0)))
        args.append(a_bias.reshape(1, kd).astype(F32))
    if bias is not None:
        in_specs.append(pl.BlockSpec((1, tn), lambda i, j: (0, j)))
        args.append(bias.reshape(1, n).astype(F32))
    return pl.pallas_call(
        functools.partial(_mm_kernel, act=act, has_abias=a_bias is not None, has_bias=bias is not None),
        grid=(m // tm, n // tn),
        in_specs=in_specs,
        out_specs=pl.BlockSpec((tm, tn), lambda i, j: (i, j)),
        out_shape=jax.ShapeDtypeStruct((m, n), out_dtype),
        compiler_params=_cparams(("parallel", "parallel")),
        name="matmul",
    )(*args)


def _swiglu_kernel(a_ref, b1_ref, b3_ref, o_ref):
    a = a_ref[...].astype(BF16)
    g = jnp.dot(a, b1_ref[...], preferred_element_type=F32)
    u = jnp.dot(a, b3_ref[...], preferred_element_type=F32)
    o_ref[...] = (g * jax.nn.sigmoid(g) * u).astype(o_ref.dtype)


def swiglu_up(a, w13):
    m, kd = a.shape
    f = w13.shape[1] // 2
    tm, tn = _whole_k_tiles(m, f, kd, a.dtype.itemsize)
    nj = f // tn
    return pl.pallas_call(
        _swiglu_kernel,
        grid=(m // tm, nj),
        in_specs=[pl.BlockSpec((tm, kd), lambda i, j: (i, 0)),
                  pl.BlockSpec((kd, tn), lambda i, j: (0, j)),
                  pl.BlockSpec((kd, tn), lambda i, j: (0, j + nj))],
        out_specs=pl.BlockSpec((tm, tn), lambda i, j: (i, j)),
        out_shape=jax.ShapeDtypeStruct((m, f), BF16),
        compiler_params=_cparams(("parallel", "parallel")),
        name="swiglu_up",
    )(a, w13, w13)


def _deepnorm(z, g, b):
    mu = jnp.mean(z, axis=-1, keepdims=True)
    zc = z - mu
    var = jnp.mean(zc * zc, axis=-1, keepdims=True)
    return zc * lax.rsqrt(var + LN_EPS) * g + b


def _mm_res_ln_kernel(a_ref, b_ref, h_ref, g_ref, beta_ref, o_ref, ob_ref, *, nj, tn, scale):
    j = pl.program_id(1)
    cols = pl.ds(pl.multiple_of(j * tn, tn), tn)
    o_ref[:, cols] = ALPHA * h_ref[...] + scale * jnp.dot(a_ref[...].astype(BF16), b_ref[...],
                                                           preferred_element_type=F32)

    @pl.when(j == nj - 1)
    def _():
        def slab(i, carry):
            rows = pl.ds(pl.multiple_of(i * LN_SLAB, LN_SLAB), LN_SLAB)
            y = _deepnorm(o_ref[rows, :], g_ref[...], beta_ref[...])
            o_ref[rows, :] = y
            ob_ref[rows, :] = y.astype(BF16)
            return carry

        lax.fori_loop(0, o_ref.shape[0] // LN_SLAB, slab, 0)


def mm_res_ln(a, b, h, g, beta, *, scale, tm=512):
    m, kd = a.shape
    n = b.shape[1]
    tm, tn = _pick(m, tm), _pick(n, MM_TN)
    assert tm % LN_SLAB == 0, (tm, LN_SLAB)
    nj = n // tn
    return pl.pallas_call(
        functools.partial(_mm_res_ln_kernel, nj=nj, tn=tn, scale=scale),
        grid=(m // tm, nj),
        in_specs=[pl.BlockSpec((tm, kd), lambda i, j: (i, 0)),
                  pl.BlockSpec((kd, tn), lambda i, j: (0, j)),
                  pl.BlockSpec((tm, tn), lambda i, j: (i, j)),
                  pl.BlockSpec((1, n), lambda i, j: (0, 0)),
                  pl.BlockSpec((1, n), lambda i, j: (0, 0))],
        out_specs=[pl.BlockSpec((tm, n), lambda i, j: (i, 0)),
                   pl.BlockSpec((tm, n), lambda i, j: (i, 0))],
        out_shape=[jax.ShapeDtypeStruct((m, n), F32), jax.ShapeDtypeStruct((m, n), BF16)],
        compiler_params=_cparams(("parallel", "arbitrary")),
        name="mm_res_ln",
    )(a, b, h, g.reshape(1, n), beta.reshape(1, n))


def _ple_kernel(t_ref, p_ref, gu_ref, pw_ref, h_ref, g_ref, beta_ref, o_ref, ob_ref):
    gate = jax.nn.sigmoid(jnp.dot(t_ref[...], gu_ref[...], preferred_element_type=F32))
    emb = jnp.dot(p_ref[...].astype(BF16), pw_ref[...], preferred_element_type=F32)
    y = _deepnorm(ALPHA * h_ref[...] + gate * emb, g_ref[...], beta_ref[...])
    o_ref[...] = y
    ob_ref[...] = y.astype(BF16)


def ple_ln(t, p, gu, pw, h, g, beta, *, tm=256):
    m, n = h.shape
    e = t.shape[1]
    tm = _pick(m, tm)
    row = lambda i: (i, 0)
    fixed = lambda i: (0, 0)
    return pl.pallas_call(
        _ple_kernel,
        grid=(m // tm,),
        in_specs=[pl.BlockSpec((tm, e), row), pl.BlockSpec((tm, e), row),
                  pl.BlockSpec((e, n), fixed), pl.BlockSpec((e, n), fixed),
                  pl.BlockSpec((tm, n), row), pl.BlockSpec((1, n), fixed), pl.BlockSpec((1, n), fixed)],
        out_specs=[pl.BlockSpec((tm, n), row), pl.BlockSpec((tm, n), row)],
        out_shape=[jax.ShapeDtypeStruct((m, n), F32), jax.ShapeDtypeStruct((m, n), BF16)],
        compiler_params=_cparams(("parallel",)),
        name="ple_ln",
    )(t, p, gu, pw, h, g.reshape(1, n), beta.reshape(1, n))


def _split2(x):
    hi = x.astype(BF16)
    return hi, (x - hi.astype(F32)).astype(BF16)


def _mm3(a, b, dims=_NN):
    dg = lambda x, y: lax.dot_general(x, y, dims, preferred_element_type=F32)
    return dg(a[0], b[0]) + (dg(a[0], b[1]) + dg(a[1], b[0]))


def _mm1(a, b, dims=_NN):
    return lax.dot_general(a, b, dims, preferred_element_type=F32)


def _split3(x):
    hi = x.astype(BF16)
    r1 = x - hi.astype(F32)
    mid = r1.astype(BF16)
    return hi, mid, (r1 - mid.astype(F32)).astype(BF16)


def _rw_mix_kernel(h_ref, hs_ref, mix_ref, o_ref):
    x = h_ref[...]
    xx = hs_ref[...] - x
    for c in range(6):
        o_ref[c] = (x + xx * mix_ref[c:c + 1, :]).astype(BF16)


def rw_mix(h, h_shift, mix, *, tm=256, tn=1024):
    m, n = h.shape
    tm, tn = _pick(m, tm), _pick(n, tn)
    return pl.pallas_call(
        _rw_mix_kernel,
        grid=(m // tm, n // tn),
        in_specs=[pl.BlockSpec((tm, tn), lambda i, j: (i, j)),
                  pl.BlockSpec((tm, tn), lambda i, j: (i, j)),
                  pl.BlockSpec((8, tn), lambda i, j: (0, j))],
        out_specs=pl.BlockSpec((6, tm, tn), lambda i, j: (0, i, j)),
        out_shape=jax.ShapeDtypeStruct((6, m, n), BF16),
        compiler_params=_cparams(("parallel", "parallel")),
        name="rw_mix",
    )(h, h_shift, jnp.pad(mix, ((0, 2), (0, 0))))


_RP_W0, _RP_A0, _RP_V0, _RP_KK, _RP_KA, _RP_RK, _RP_LNG, _RP_LNB = range(8)


def _each(f, *lists):
    return [f(*xs) for xs in zip(*lists)]


def _rw_chunk_local(a_t, r_t, b_h, k_h, v, p_last, strict, incl, eye):
    rnd = lambda xs: _each(lambda z: z.astype(BF16), xs)
    nt = lambda a, b: _mm1(a, b, _NT)
    tn = lambda a, b: _mm1(a, b, _TN)
    at, rt, bh, kh, vs = rnd(a_t), rnd(r_t), rnd(b_h), rnd(k_h), rnd(v)
    m_ab = _each(lambda z: jnp.where(strict, z, 0.0), _each(nt, at, bh))
    m_ak = _each(lambda z: jnp.where(strict, z, 0.0), _each(nt, at, kh))
    m_rb = _each(lambda z: jnp.where(incl, z, 0.0), _each(nt, rt, bh))
    m_rk = _each(lambda z: jnp.where(incl, z, 0.0), _each(nt, rt, kh))
    x = _each(lambda z: eye + z, m_ab)
    pw = m_ab
    for _ in range(max(1, int(math.ceil(math.log2(a_t[0].shape[0]))) - 1)):
        ps = rnd(pw)
        pw = _each(_mm1, ps, ps)
        x = _each(lambda z, d: z + d, x, _each(_mm1, rnd(x), rnd(pw)))
    xs = rnd(x)
    ax = rnd(_each(_mm1, xs, at))
    u0 = rnd(_each(_mm1, xs, rnd(_each(_mm1, rnd(m_ak), vs))))
    rb = rnd(m_rb)
    r_hat = _each(lambda z, d: z + d, r_t, _each(_mm1, rb, ax))
    y0 = _each(lambda z, d: z + d, _each(_mm1, rb, u0), _each(_mm1, rnd(m_rk), vs))
    t_m = _each(lambda z, p: (eye + z) * p, _each(tn, ax, bh), p_last)
    s_add = _each(lambda z, d, p: (z + d) * p, _each(tn, u0, bh), _each(tn, vs, kh), p_last)
    return r_hat, y0, t_m, s_add


def _rw_rec_kernel(*refs, has_vres, n_chunks):
    if has_vres:
        (r_ref, k_ref, v_ref, vf_ref, hw_ref, ha_ref, hg_ref, hv_ref, w2_ref, a2_ref, g2_ref, v2_ref,
         rp_ref, o_ref, s_ref, at_s, rt_s, bh_s, kh_s, v_s, g_s, bon_s, rh_s, y0_s, tm_s, sa_s) = refs
    else:
        (r_ref, k_ref, v_ref, hw_ref, ha_ref, hg_ref, w2_ref, a2_ref, g2_ref,
         rp_ref, o_ref, s_ref, at_s, rt_s, bh_s, kh_s, v_s, g_s, bon_s, rh_s, y0_s, tm_s, sa_s) = refs
    n = RW_HEAD
    c = RW_CHUNK
    heads = RW_LANES // n

    @pl.when(pl.program_id(1) == 0)
    def _():
        s_ref[...] = jnp.zeros_like(s_ref)

    rp = rp_ref[...]
    row = lambda i: rp[i:i + 1, :]
    lane = lax.broadcasted_iota(jnp.int32, (1, RW_LANES), 1)

    def per_head_sum(z):
        out = jnp.zeros_like(z)
        for j in range(heads):
            mine = (lane >= j * n) & (lane < (j + 1) * n)
            out = jnp.where(mine, jnp.sum(jnp.where(mine, z, 0.0), axis=-1, keepdims=True), out)
        return out

    zw = row(_RP_W0) + jnp.dot(hw_ref[...], w2_ref[...], preferred_element_type=F32)
    nz = -zw
    softplus = jnp.maximum(nz, 0.0) + jnp.log1p(jnp.exp(-jnp.abs(nz)))
    lw = -jnp.exp(-softplus - 0.5)
    a = jax.nn.sigmoid(row(_RP_A0) + jnp.dot(ha_ref[...], a2_ref[...], preferred_element_type=F32))
    g_s[...] = jnp.dot(hg_ref[...], g2_ref[...], preferred_element_type=F32)
    r = r_ref[...]
    k = k_ref[...]
    v = v_ref[...]
    if has_vres:
        mixv = jax.nn.sigmoid(row(_RP_V0) + jnp.dot(hv_ref[...], v2_ref[...], preferred_element_type=F32))
        v = v + (vf_ref[...] - v) * mixv
    v_s[...] = v
    kk = k * row(_RP_KK)
    kk = kk * lax.rsqrt(jnp.maximum(per_head_sum(kk * kk), 1e-24))
    k = k * (1.0 + (a - 1.0) * row(_RP_KA))
    bon_s[...] = per_head_sum(r * k * row(_RP_RK)) * v

    ti = lax.broadcasted_iota(jnp.int32, (c, c), 0)
    si = lax.broadcasted_iota(jnp.int32, (c, c), 1)
    strict = si < ti
    incl = si <= ti
    tri_incl = incl.astype(BF16)
    eye = (si == ti).astype(F32)

    p_last = []
    for ci in range(n_chunks):
        rows = slice(ci * c, (ci + 1) * c)
        lw_c = lw[rows]
        cum = sum(jnp.dot(tri_incl, part, preferred_element_type=F32) for part in _split3(lw_c))
        p_in = jnp.exp(cum)
        p_inv = jnp.exp(-cum)
        at_s[rows, :] = -kk[rows] * jnp.exp(cum - lw_c)
        rt_s[rows, :] = r[rows] * p_in
        bh_s[rows, :] = kk[rows] * a[rows] * p_inv
        kh_s[rows, :] = k[rows] * p_inv
        p_last.append(p_in[c - 1:c, :])

    where = [(slice(ci * c, (ci + 1) * c), slice(j * n, (j + 1) * n)) for ci in range(n_chunks) for j in range(heads)]
    r_hat, y0, t_m, s_add = _rw_chunk_local(
        [at_s[w] for w in where], [rt_s[w] for w in where], [bh_s[w] for w in where], [kh_s[w] for w in where],
        [v_s[w] for w in where], [p_last[ci][:, j * n:(j + 1) * n] for ci in range(n_chunks) for j in range(heads)],
        strict, incl, eye)
    for i, w in enumerate(where):
        rh_s[w] = r_hat[i]
        y0_s[w] = y0[i]
        tm_s[i] = t_m[i]
        sa_s[i] = s_add[i]

    s = [s_ref[j] for j in range(heads)]
    for ci in range(n_chunks):
        rows = slice(ci * c, (ci + 1) * c)
        for j in range(heads):
            ln = slice(j * n, (j + 1) * n)
            ss = _split2(s[j])
            y = _mm3(_split2(rh_s[rows, ln]), ss, _NT) + y0_s[rows, ln]
            s[j] = _mm3(ss, _split2(tm_s[ci * heads + j])) + sa_s[ci * heads + j]
            mu = jnp.mean(y, axis=-1, keepdims=True)
            yc = y - mu
            var = jnp.mean(yc * yc, axis=-1, keepdims=True)
            yn = yc * lax.rsqrt(var + RW_GN_EPS) * row(_RP_LNG)[:, ln] + row(_RP_LNB)[:, ln]
            o_ref[rows, ln] = ((yn + bon_s[rows, ln]) * g_s[rows, ln]).astype(o_ref.dtype)
    for j in range(heads):
        s_ref[j] = s[j]


def rw_recurrence(r, k, v, v_first, hw, ha, hg, hv, w2, a2, g2, v2, rowp, *, tb=512):
    t, d = r.shape
    tb = _pick(t, tb)
    has_vres = v_first is not None
    tile = pl.BlockSpec((tb, RW_LANES), lambda hb, i: (i, hb))
    lora = lambda w: pl.BlockSpec((tb, w), lambda hb, i: (i, 0))
    wcol = lambda w: pl.BlockSpec((w, RW_LANES), lambda hb, i: (0, hb))
    lw_, la_, lg_ = hw.shape[1], ha.shape[1], hg.shape[1]
    if has_vres:
        args = [r, k, v, v_first, hw, ha, hg, hv, w2, a2, g2, v2, rowp]
        in_specs = [tile, tile, tile, tile, lora(lw_), lora(la_), lora(lg_), lora(hv.shape[1]),
                    wcol(lw_), wcol(la_), wcol(lg_), wcol(hv.shape[1]), wcol(8)]
    else:
        args = [r, k, v, hw, ha, hg, w2, a2, g2, rowp]
        in_specs = [tile, tile, tile, lora(lw_), lora(la_), lora(lg_), wcol(lw_), wcol(la_), wcol(lg_), wcol(8)]
    heads = RW_LANES // RW_HEAD
    n_chunks = tb // RW_CHUNK
    per_chunk = pltpu.VMEM((n_chunks * heads, RW_HEAD, RW_HEAD), F32)
    return pl.pallas_call(
        functools.partial(_rw_rec_kernel, has_vres=has_vres, n_chunks=n_chunks),
        grid=(d // RW_LANES, t // tb),
        in_specs=in_specs,
        out_specs=tile,
        out_shape=jax.ShapeDtypeStruct((t, d), BF16),
        scratch_shapes=[pltpu.VMEM((heads, RW_HEAD, RW_HEAD), F32)]
        + [pltpu.VMEM((tb, RW_LANES), F32)] * 9 + [per_chunk, per_chunk],
        compiler_params=_cparams(("parallel", "arbitrary")),
        name="rw_recurrence",
    )(*args)


def _pad_cols(w, mult=LANES):
    p = (-w.shape[-1]) % mult
    return jnp.pad(w, ((0, 0), (0, p))) if p else w


def _pad_rows(w, mult=LANES):
    p = (-w.shape[0]) % mult
    return jnp.pad(w, ((0, p), (0, 0))) if p else w


def rwkv7_time_mix(h, v_first, mix, w_rkv, w_o, w0, w1, w2, a0, a1, a2, g1, g2, k_k, k_a, r_k, lnx_g, lnx_b, v_lora):
    t, d = h.shape
    h_shift = jnp.concatenate([jnp.zeros((1, d), h.dtype), h[:-1]], axis=0)
    xm = rw_mix(h, h_shift, mix)
    wb = w_rkv.astype(BF16)
    r = matmul(xm, wb[0], lead=0)
    k = matmul(xm, wb[1], lead=1)
    v = matmul(xm, wb[2], lead=2)
    hw = matmul(xm, _pad_cols(w1).astype(BF16), lead=3, out_dtype=BF16, act=jnp.tanh)
    ha = matmul(xm, _pad_cols(a1).astype(BF16), lead=4, out_dtype=BF16)
    hg = matmul(xm, _pad_cols(g1).astype(BF16), lead=5, out_dtype=BF16, act=jax.nn.sigmoid)
    zero = jnp.zeros((d,), F32)
    rowp = jnp.stack([w0, a0, zero if v_lora is None else v_lora[0], k_k, k_a, r_k.reshape(d), lnx_g, lnx_b])
    if v_lora is None:
        out = rw_recurrence(r, k, v, None, hw, ha, hg, None, _pad_rows(w2).astype(BF16),
                            _pad_rows(a2).astype(BF16), _pad_rows(g2).astype(BF16), None, rowp)
        v_first = v
    else:
        hv = matmul(xm, _pad_cols(v_lora[1]).astype(BF16), lead=2, out_dtype=BF16)
        out = rw_recurrence(r, k, v, v_first, hw, ha, hg, hv, _pad_rows(w2).astype(BF16),
                            _pad_rows(a2).astype(BF16), _pad_rows(g2).astype(BF16),
                            _pad_rows(v_lora[2]).astype(BF16), rowp)
    return out, v_first


def _nsa_kernel(q_ref, gate_ref, kc_ref, vc_ref, ks_ref, vs_ref, *rest, n_heads_total):
    kw_refs = rest[0:5]
    vw_refs = rest[5:10]
    o_ref, m_s, l_s, acc_s, oc_s, imp_s = rest[10:16]
    g = pl.program_id(0)
    qb = pl.program_id(1)
    hpg, dk, qn = NSA_HPG, NSA_DK, Q_BLOCK
    t0 = qb * qn
    hrows = [slice(h * qn, (h + 1) * qn) for h in range(hpg)]

    q2 = q_ref[...]
    qs = jnp.concatenate([q2[:, h * dk:(h + 1) * dk] for h in range(hpg)], axis=0)
    tq = t0 + lax.broadcasted_iota(jnp.int32, (qn, 1), 0)
    slopes = [jnp.exp2(-8.0 * jnp.full((1, 1), g * hpg + h + 1, jnp.int32).astype(F32) / n_heads_total)
              for h in range(hpg)]


    n_cp = kc_ref.shape[1]
    n_sel = (n_cp * CMP_STRIDE) // SEL_BLOCK
    ratio = SEL_BLOCK // CMP_STRIDE
    n_ov = CMP_BLOCK // CMP_STRIDE

    def compressed(ncols):
        kc = kc_ref[0, :ncols, :]
        s_all = lax.dot_general(qs, kc, _NT, preferred_element_type=F32)
        pos_c = lax.broadcasted_iota(jnp.int32, (1, ncols), 1) * CMP_STRIDE + (CMP_BLOCK - 1)
        d_c = (tq - pos_c).astype(F32)
        hide_c = jnp.where(d_c >= 0.0, 0.0, NEG_INF)
        p_grp = jnp.zeros((qn, ncols), F32)
        p_heads = []
        for h in range(hpg):
            s = s_all[hrows[h]] + (hide_c - slopes[h] * d_c)
            m = jnp.max(s, axis=-1, keepdims=True)
            p = jnp.exp(s - m)
            l = jnp.sum(p, axis=-1, keepdims=True)
            p = p * jnp.where(m > 0.5 * NEG_INF, 1.0 / l, 0.0)
            p_grp = p_grp + p
            p_heads.append(p.astype(BF16))
        oc_s[...] = jnp.dot(jnp.concatenate(p_heads, axis=0), vc_ref[0, :ncols, :], preferred_element_type=F32)
        ji = lax.broadcasted_iota(jnp.int32, (n_sel, ncols), 0)
        ci = lax.broadcasted_iota(jnp.int32, (n_sel, ncols), 1)
        off = ci - ratio * ji
        cnt = jnp.maximum(jnp.minimum(jnp.minimum(off + n_ov, ratio - off), jnp.minimum(n_ov, ratio)), 0)
        cnt = cnt.astype(F32).astype(BF16)
        imp_s[...] = sum(lax.dot_general(cnt, part, _NT, preferred_element_type=F32) for part in _split3(p_grp))

    step_c = min(CMP_COLS, n_cp)
    n_var = n_cp // step_c
    need = jnp.minimum(((t0 + qn) // CMP_STRIDE + step_c - 1) // step_c, n_var)
    for b in range(1, n_var + 1):
        @pl.when(need == b)
        def _(b=b):
            compressed(b * step_c)

    tq_l = t0 + lax.broadcasted_iota(jnp.int32, (1, qn), 1)
    cur = tq_l // SEL_BLOCK
    sj = lax.broadcasted_iota(jnp.int32, (n_sel, qn), 0)
    forced = (sj == 0) | (sj == cur) | (sj == cur - 1)
    work = jnp.where(sj > cur, -1.0, jnp.where(forced, FORCE_SCORE, imp_s[...]))
    sel = jnp.zeros((n_sel, qn), F32)
    sjf = sj.astype(F32)
    for _ in range(min(SEL_TOP_N, n_sel)):
        mx = jnp.max(work, axis=0, keepdims=True)
        first = jnp.min(jnp.where(work == mx, sjf, float(n_sel)), axis=0, keepdims=True)
        pick = sjf == first
        sel = jnp.where(pick, 1.0, sel)
        work = jnp.where(pick, -2.0, work)
    sel_b = sel.astype(BF16)

    m_s[...] = jnp.full_like(m_s, NEG_INF)
    l_s[...] = jnp.zeros_like(l_s)
    acc_s[...] = jnp.zeros_like(acc_s)
    n_keys = ks_ref.shape[0]
    tk = min(SEL_TILE, n_keys)
    bpt = tk // SEL_BLOCK

    def sweep(i, diagonal):
        k0 = pl.multiple_of(i * tk, tk)
        kt = ks_ref[pl.ds(k0, tk), :]
        vt = vs_ref[pl.ds(k0, tk), :]
        st = lax.dot_general(qs, kt, _NT, preferred_element_type=F32)
        bi = lax.broadcasted_iota(jnp.int32, (n_sel, tk), 0)
        ki = lax.broadcasted_iota(jnp.int32, (n_sel, tk), 1)
        expand = (bi == i * bpt + ki // SEL_BLOCK).astype(F32).astype(BF16)
        chosen = lax.dot_general(sel_b, expand, _TN, preferred_element_type=F32)
        d_s = (tq - (k0 + lax.broadcasted_iota(jnp.int32, (1, tk), 1))).astype(F32)
        hide = (chosen - 1.0) * (-NEG_INF)
        if diagonal:
            hide = jnp.where(d_s >= 0.0, hide, NEG_INF)
        p_heads, alphas = [], []
        for h in range(hpg):
            s = st[hrows[h]] + (hide - slopes[h] * d_s)
            m_old = m_s[hrows[h], :]
            m_new = jnp.maximum(m_old, jnp.max(s, axis=-1, keepdims=True))
            alpha = jnp.exp(m_old - m_new)
            p = jnp.exp(s - m_new)
            l_s[hrows[h], :] = alpha * l_s[hrows[h], :] + jnp.sum(p, axis=-1, keepdims=True)
            m_s[hrows[h], :] = m_new
            alphas.append(alpha)
            p_heads.append(p.astype(BF16))
        pv = jnp.dot(jnp.concatenate(p_heads, axis=0), vt, preferred_element_type=F32)
        acc_s[...] = jnp.concatenate(alphas, axis=0) * acc_s[...] + pv

    n_below = t0 // tk
    block_used = jnp.max(sel, axis=1, keepdims=True)
    block_id = lax.broadcasted_iota(jnp.int32, (n_sel, 1), 0)

    def below(i, carry):
        in_tile = (block_id >= i * bpt) & (block_id < (i + 1) * bpt)
        used = jnp.max(jnp.where(in_tile, block_used, 0.0))

        @pl.when(used > 0.5)
        def _():
            sweep(i, False)

        return carry

    lax.fori_loop(0, n_below, below, 0)
    sweep(n_below, True)
    o_s = acc_s[...] * (1.0 / l_s[...])

    kw = jnp.concatenate([r[...] for r in kw_refs], axis=0)
    vw = jnp.concatenate([r[...] for r in vw_refs], axis=0)
    sw = lax.dot_general(qs, kw, _NT, preferred_element_type=F32)
    pos_w = t0 - WINDOW + lax.broadcasted_iota(jnp.int32, (1, WINDOW + qn), 1)
    d_wi = tq - pos_w
    d_w = d_wi.astype(F32)
    hide_w = jnp.where((d_wi >= 0) & (d_wi < WINDOW) & (pos_w >= 0), 0.0, NEG_INF)
    p_heads, inv_l = [], []
    for h in range(hpg):
        s = sw[hrows[h]] + (hide_w - slopes[h] * d_w)
        p = jnp.exp(s - jnp.max(s, axis=-1, keepdims=True))
        inv_l.append(1.0 / jnp.sum(p, axis=-1, keepdims=True))
        p_heads.append(p.astype(BF16))
    o_w = jnp.dot(jnp.concatenate(p_heads, axis=0), vw, preferred_element_type=F32) * jnp.concatenate(inv_l, axis=0)

    gate = gate_ref[...]
    o_c = oc_s[...]
    for h in range(hpg):
        g_c, g_s, g_w = (gate[:, b * hpg + h:b * hpg + h + 1] for b in range(3))
        o = g_c * o_c[hrows[h]] + g_s * o_s[hrows[h]] + g_w * o_w[hrows[h]]
        o_ref[:, h * dk:(h + 1) * dk] = o.astype(o_ref.dtype)


def nsa_attention_core(q, gate, k_cmp, v_cmp, kvb, n_groups):
    t, d = q.shape
    gq = NSA_HPG * NSA_DK
    n_qb = t // Q_BLOCK
    nwb = WINDOW // Q_BLOCK
    n_cp = k_cmp.shape[1]
    full = lambda j: pl.BlockSpec((t, NSA_DK), lambda g, qb, j=j: (0, j * n_groups + g))

    def win(j, i):
        return pl.BlockSpec((Q_BLOCK, NSA_DK), lambda g, qb, j=j, i=i: (jnp.maximum(qb - nwb + i, 0), j * n_groups + g))

    in_specs = [pl.BlockSpec((Q_BLOCK, gq), lambda g, qb: (qb, g)),
                pl.BlockSpec((Q_BLOCK, LANES), lambda g, qb: (qb, g)),
                pl.BlockSpec((1, n_cp, NSA_DK), lambda g, qb: (g, 0, 0)),
                pl.BlockSpec((1, n_cp, NSA_DK), lambda g, qb: (g, 0, 0)),
                full(2), full(3)]
    in_specs += [win(4, i) for i in range(nwb + 1)] + [win(5, i) for i in range(nwb + 1)]
    rows = NSA_HPG * Q_BLOCK
    return pl.pallas_call(
        functools.partial(_nsa_kernel, n_heads_total=n_groups * NSA_HPG),
        grid=(n_groups, n_qb),
        in_specs=in_specs,
        out_specs=pl.BlockSpec((Q_BLOCK, gq), lambda g, qb: (qb, g)),
        out_shape=jax.ShapeDtypeStruct((t, d), BF16),
        scratch_shapes=[pltpu.VMEM((rows, 1), F32), pltpu.VMEM((rows, 1), F32),
                        pltpu.VMEM((rows, NSA_DK), F32), pltpu.VMEM((rows, NSA_DK), F32),
                        pltpu.VMEM((t // SEL_BLOCK, Q_BLOCK), F32)],
        compiler_params=_cparams(("parallel", "arbitrary")),
        name="nsa_attention",
    )(q, gate, k_cmp, v_cmp, kvb, kvb, *([kvb] * (2 * (nwb + 1))))


def nsa_shared_kv(hb, w_kv, cmp_pos, cmp_w1, cmp_b1, cmp_w2):
    t = hb.shape[0]
    n_groups = w_kv.shape[1] // (6 * NSA_DK)
    kv = matmul(hb, w_kv.astype(BF16))
    n_chunk = t // CMP_STRIDE
    cmp = []
    for j in range(2):
        z = kv[:, j * n_groups * NSA_DK:(j + 1) * n_groups * NSA_DK]
        ch = z.reshape(n_chunk, CMP_STRIDE, n_groups, NSA_DK).transpose(2, 0, 1, 3).reshape(n_groups, n_chunk, CMP_STRIDE * NSA_DK)
        nxt = jnp.concatenate([ch[:, 1:], jnp.zeros_like(ch[:, :1])], axis=1)
        flat = jnp.concatenate([ch, nxt], axis=-1).reshape(n_groups * n_chunk, CMP_BLOCK * NSA_DK)
        hid = matmul(flat, cmp_w1[j].astype(BF16), out_dtype=BF16, act=jax.nn.silu,
                     a_bias=cmp_pos[j].reshape(-1), bias=cmp_b1[j])
        out = matmul(hid, cmp_w2[j].astype(BF16))
        cmp.append(out.reshape(n_groups, n_chunk, NSA_DK).astype(BF16))
    return cmp[0], cmp[1], kv.astype(BF16)


def nsa_layer(hb, shared, w_qg, b_g):
    k_cmp, v_cmp, kvb = shared
    n_groups = k_cmp.shape[0]
    n_heads = n_groups * NSA_HPG
    dq = n_heads * NSA_DK
    q = matmul(hb, w_qg[:, :dq].astype(BF16), out_dtype=BF16, act=lambda z: z * (NSA_DK ** -0.5))
    pad = LANES - 3 * NSA_HPG
    w_g = w_qg[:, dq:].reshape(-1, n_groups, NSA_HPG, 3).transpose(0, 1, 3, 2).reshape(-1, n_groups, 3 * NSA_HPG)
    w_g = jnp.pad(w_g, ((0, 0), (0, 0), (0, pad))).reshape(-1, n_groups * LANES)
    b_gp = b_g.reshape(n_groups, NSA_HPG, 3).transpose(0, 2, 1).reshape(n_groups, 3 * NSA_HPG)
    b_gp = jnp.pad(b_gp, ((0, 0), (0, pad))).reshape(-1)
    gate = matmul(hb, w_g.astype(BF16), act=jax.nn.sigmoid, bias=b_gp)
    return nsa_attention_core(q, gate, k_cmp, v_cmp, kvb, n_groups)


def kernel(x, p, ln_g, ln_b, ffn1_w13, ffn1_w2, ffn2_w13, ffn2_w2, ple_w, ple_gate_down, ple_gate_up, rw_mix, rw_w_rkv, rw_w_o, rw_w0, rw_w1, rw_w2, rw_a0, rw_a1, rw_a2, rw_g1, rw_g2, rw_k_k, rw_k_a, rw_r_k, rw_lnx_g, rw_lnx_b, rw_v0, rw_v1, rw_v2, nsa_w_kv, cmp_pos, cmp_w1, cmp_b1, cmp_w2, nsa_w_qg, nsa_b_g, nsa_w_o):
    batch, t, d = x.shape
    depth = ln_g.shape[0]
    n_a = rw_mix.shape[0]
    outs = []
    for bi in range(batch):
        h = x[bi]
        hb = h.astype(BF16)
        v_first = None
        shared = None
        for i in range(depth):
            mid = swiglu_up(hb, ffn1_w13[i].astype(BF16))
            h, hb = mm_res_ln(mid, ffn1_w2[i].astype(BF16), h, ln_g[i, 0], ln_b[i, 0], scale=0.5)
            if i < n_a:
                v_lora = None if i == 0 else (rw_v0[i - 1], rw_v1[i - 1], rw_v2[i - 1])
                mix_pre, v_first = rwkv7_time_mix(
                    h, v_first, rw_mix[i], rw_w_rkv[i], rw_w_o[i], rw_w0[i], rw_w1[i], rw_w2[i], rw_a0[i],
                    rw_a1[i], rw_a2[i], rw_g1[i], rw_g2[i], rw_k_k[i], rw_k_a[i], rw_r_k[i], rw_lnx_g[i],
                    rw_lnx_b[i], v_lora)
                w_o = rw_w_o[i]
            else:
                j = i - n_a
                mix_pre = nsa_layer(hb, shared, nsa_w_qg[j], nsa_b_g[j])
                w_o = nsa_w_o[j]
            h, hb = mm_res_ln(mix_pre, w_o.astype(BF16), h, ln_g[i, 1], ln_b[i, 1], scale=1.0)
            mid = swiglu_up(hb, ffn2_w13[i].astype(BF16))
            h, hb = mm_res_ln(mid, ffn2_w2[i].astype(BF16), h, ln_g[i, 2], ln_b[i, 2], scale=0.5)
            tdown = matmul(hb, ple_gate_down[i].astype(BF16), out_dtype=BF16)
            h, hb = ple_ln(tdown, p[i, bi], ple_gate_up[i].astype(BF16), ple_w[i].astype(BF16), h,
                           ln_g[i, 3], ln_b[i, 3])
            if i == n_a - 1:
                shared = nsa_shared_kv(hb, nsa_w_kv, cmp_pos, cmp_w1, cmp_b1, cmp_w2)
        outs.append(h)
    return jnp.stack(outs, axis=0)
```

```python
import functools
import math

import jax
import jax.numpy as jnp
from jax import lax
from jax.experimental import pallas as pl
from jax.experimental.pallas import tpu as pltpu

F32 = jnp.float32
BF16 = jnp.bfloat16

V7X_VMEM_BYTES = 64 * 1024 * 1024
VMEM_LIMIT = V7X_VMEM_BYTES - 8 * 1024 * 1024
LANES = 128
SUBLANES = 8

MM_TM_BYTES = 8 * 1024 * 1024
MM_TN = 512

LN_EPS = 1e-5
LN_SLAB = 128
DEPTH_TOTAL = 4
ALPHA = (2.0 * DEPTH_TOTAL) ** 0.25

RW_HEAD = 64
RW_GN_EPS = 64e-5
RW_CHUNK = 64
RW_LANES = 256

NSA_DK = 128
NSA_HPG = 8
CMP_BLOCK = 32
CMP_STRIDE = 16
SEL_BLOCK = 64
SEL_TOP_N = 16
WINDOW = 512
Q_BLOCK = 128
FORCE_SCORE = 1e4
NEG_INF = -1e30
SEL_TILE = 512
CMP_COLS = 256

_NN = (((1,), (0,)), ((), ()))
_NT = (((1,), (1,)), ((), ()))
_TN = (((0,), (0,)), ((), ()))


def _cparams(sem):
    return pltpu.CompilerParams(dimension_semantics=sem, vmem_limit_bytes=VMEM_LIMIT)


def _pick(n, pref):
    if n <= pref:
        return n
    t = pref
    while n % t:
        t //= 2
    return t


def _mm_kernel(*refs, act, has_abias, has_bias):
    a_ref, b_ref = refs[0], refs[1]
    i = 2
    abias_ref = bias_ref = None
    if has_abias:
        abias_ref = refs[i]; i += 1
    if has_bias:
        bias_ref = refs[i]; i += 1
    o_ref = refs[i]
    a = a_ref[...]
    if has_abias:
        a = a.astype(F32) + abias_ref[...]
    z = jnp.dot(a.astype(BF16), b_ref[...], preferred_element_type=F32)
    if has_bias:
        z = z + bias_ref[...]
    if act is not None:
        z = act(z)
    o_ref[...] = z.astype(o_ref.dtype)


def _whole_k_tiles(m, n, kd, a_itemsize):
    assert kd * MM_TN * 2 * 2 <= VMEM_LIMIT // 4, kd
    return _pick(m, MM_TM_BYTES // (kd * a_itemsize)), _pick(n, MM_TN)


def matmul(a, b, *, lead=None, out_dtype=F32, act=None, a_bias=None, bias=None):
    m, kd = a.shape[-2:]
    _, n = b.shape
    tm, tn = _whole_k_tiles(m, n, kd, a.dtype.itemsize)
    if lead is None:
        a_spec = pl.BlockSpec((tm, kd), lambda i, j: (i, 0))
    else:
        a_spec = pl.BlockSpec((None, tm, kd), lambda i, j: (lead, i, 0))
    in_specs = [a_spec, pl.BlockSpec((kd, tn), lambda i, j: (0, j))]
    args = [a, b]
    if a_bias is not None:
        in_specs.append(pl.BlockSpec((1, kd), lambda i, j: (0, 0)))
        args.append(a_bias.reshape(1, kd).astype(F32))
    if bias is not None:
        in_specs.append(pl.BlockSpec((1, tn), lambda i, j: (0, j)))
        args.append(bias.reshape(1, n).astype(F32))
    return pl.pallas_call(
        functools.partial(_mm_kernel, act=act, has_abias=a_bias is not None, has_bias=bias is not None),
        grid=(m // tm, n // tn),
        in_specs=in_specs,
        out_specs=pl.BlockSpec((tm, tn), lambda i, j: (i, j)),
        out_shape=jax.ShapeDtypeStruct((m, n), out_dtype),
        compiler_params=_cparams(("parallel", "parallel")),
        name="matmul",
    )(*args)


def _swiglu_kernel(a_ref, b1_ref, b3_ref, o_ref):
    a = a_ref[...].astype(BF16)
    g = jnp.dot(a, b1_ref[...], preferred_element_type=F32)
    u = jnp.dot(a, b3_ref[...], preferred_element_type=F32)
    o_ref[...] = (g * jax.nn.sigmoid(g) * u).astype(o_ref.dtype)


def swiglu_up(a, w13):
    m, kd = a.shape
    f = w13.shape[1] // 2
    tm, tn = _whole_k_tiles(m, f, kd, a.dtype.itemsize)
    nj = f // tn
    return pl.pallas_call(
        _swiglu_kernel,
        grid=(m // tm, nj),
        in_specs=[pl.BlockSpec((tm, kd), lambda i, j: (i, 0)),
                  pl.BlockSpec((kd, tn), lambda i, j: (0, j)),
                  pl.BlockSpec((kd, tn), lambda i, j: (0, j + nj))],
        out_specs=pl.BlockSpec((tm, tn), lambda i, j: (i, j)),
        out_shape=jax.ShapeDtypeStruct((m, f), BF16),
        compiler_params=_cparams(("parallel", "parallel")),
        name="swiglu_up",
    )(a, w13, w13)


def _deepnorm(z, g, b):
    mu = jnp.mean(z, axis=-1, keepdims=True)
    zc = z - mu
    var = jnp.mean(zc * zc, axis=-1, keepdims=True)
    return zc * lax.rsqrt(var + LN_EPS) * g + b


def _mm_res_ln_kernel(a_ref, b_ref, h_ref, g_ref, beta_ref, o_ref, ob_ref, *, nj, tn, scale):
    j = pl.program_id(1)
    cols = pl.ds(pl.multiple_of(j * tn, tn), tn)
    o_ref[:, cols] = ALPHA * h_ref[...] + scale * jnp.dot(a_ref[...].astype(BF16), b_ref[...],
                                                           preferred_element_type=F32)

    @pl.when(j == nj - 1)
    def _():
        def slab(i, carry):
            rows = pl.ds(pl.multiple_of(i * LN_SLAB, LN_SLAB), LN_SLAB)
            y = _deepnorm(o_ref[rows, :], g_ref[...], beta_ref[...])
            o_ref[rows, :] = y
            ob_ref[rows, :] = y.astype(BF16)
            return carry

        lax.fori_loop(0, o_ref.shape[0] // LN_SLAB, slab, 0)


def mm_res_ln(a, b, h, g, beta, *, scale, tm=512):
    m, kd = a.shape
    n = b.shape[1]
    tm, tn = _pick(m, tm), _pick(n, MM_TN)
    assert tm % LN_SLAB == 0, (tm, LN_SLAB)
    nj = n // tn
    return pl.pallas_call(
        functools.partial(_mm_res_ln_kernel, nj=nj, tn=tn, scale=scale),
        grid=(m // tm, nj),
        in_specs=[pl.BlockSpec((tm, kd), lambda i, j: (i, 0)),
                  pl.BlockSpec((kd, tn), lambda i, j: (0, j)),
                  pl.BlockSpec((tm, tn), lambda i, j: (i, j)),
                  pl.BlockSpec((1, n), lambda i, j: (0, 0)),
                  pl.BlockSpec((1, n), lambda i, j: (0, 0))],
        out_specs=[pl.BlockSpec((tm, n), lambda i, j: (i, 0)),
                   pl.BlockSpec((tm, n), lambda i, j: (i, 0))],
        out_shape=[jax.ShapeDtypeStruct((m, n), F32), jax.ShapeDtypeStruct((m, n), BF16)],
        compiler_params=_cparams(("parallel", "arbitrary")),
        name="mm_res_ln",
    )(a, b, h, g.reshape(1, n), beta.reshape(1, n))


def _ple_kernel(t_ref, p_ref, gu_ref, pw_ref, h_ref, g_ref, beta_ref, o_ref, ob_ref):
    gate = jax.nn.sigmoid(jnp.dot(t_ref[...], gu_ref[...], preferred_element_type=F32))
    emb = jnp.dot(p_ref[...].astype(BF16), pw_ref[...], preferred_element_type=F32)
    y = _deepnorm(ALPHA * h_ref[...] + gate * emb, g_ref[...], beta_ref[...])
    o_ref[...] = y
    ob_ref[...] = y.astype(BF16)


def ple_ln(t, p, gu, pw, h, g, beta, *, tm=256):
    m, n = h.shape
    e = t.shape[1]
    tm = _pick(m, tm)
    row = lambda i: (i, 0)
    fixed = lambda i: (0, 0)
    return pl.pallas_call(
        _ple_kernel,
        grid=(m // tm,),
        in_specs=[pl.BlockSpec((tm, e), row), pl.BlockSpec((tm, e), row),
                  pl.BlockSpec((e, n), fixed), pl.BlockSpec((e, n), fixed),
                  pl.BlockSpec((tm, n), row), pl.BlockSpec((1, n), fixed), pl.BlockSpec((1, n), fixed)],
        out_specs=[pl.BlockSpec((tm, n), row), pl.BlockSpec((tm, n), row)],
        out_shape=[jax.ShapeDtypeStruct((m, n), F32), jax.ShapeDtypeStruct((m, n), BF16)],
        compiler_params=_cparams(("parallel",)),
        name="ple_ln",
    )(t, p, gu, pw, h, g.reshape(1, n), beta.reshape(1, n))


def _split2(x):
    hi = x.astype(BF16)
    return hi, (x - hi.astype(F32)).astype(BF16)


def _mm3(a, b, dims=_NN):
    dg = lambda x, y: lax.dot_general(x, y, dims, preferred_element_type=F32)
    return dg(a[0], b[0]) + (dg(a[0], b[1]) + dg(a[1], b[0]))


def _mm1(a, b, dims=_NN):
    return lax.dot_general(a, b, dims, preferred_element_type=F32)


def _split3(x):
    hi = x.astype(BF16)
    r1 = x - hi.astype(F32)
    mid = r1.astype(BF16)
    return hi, mid, (r1 - mid.astype(F32)).astype(BF16)


def _rw_mix_kernel(h_ref, above_ref, mix_ref, o_ref):
    x = h_ref[...]
    first = jnp.where(pl.program_id(0) > 0, above_ref[SUBLANES - 1:SUBLANES, :], 0.0)
    row = lax.broadcasted_iota(jnp.int32, x.shape, 0)
    xx = jnp.where(row == 0, first, pltpu.roll(x, 1, axis=0)) - x
    for c in range(6):
        o_ref[c] = (x + xx * mix_ref[c:c + 1, :]).astype(BF16)


def rw_mix(h, mix, *, tm=256, tn=1024):
    m, n = h.shape
    tm, tn = _pick(m, tm), _pick(n, tn)
    per = tm // SUBLANES
    return pl.pallas_call(
        _rw_mix_kernel,
        grid=(m // tm, n // tn),
        in_specs=[pl.BlockSpec((tm, tn), lambda i, j: (i, j)),
                  pl.BlockSpec((SUBLANES, tn), lambda i, j: (jnp.maximum(i * per - 1, 0), j)),
                  pl.BlockSpec((SUBLANES, tn), lambda i, j: (0, j))],
        out_specs=pl.BlockSpec((6, tm, tn), lambda i, j: (0, i, j)),
        out_shape=jax.ShapeDtypeStruct((6, m, n), BF16),
        compiler_params=_cparams(("parallel", "parallel")),
        name="rw_mix",
    )(h, h, jnp.pad(mix, ((0, SUBLANES - mix.shape[0]), (0, 0))))


_RP_W0, _RP_A0, _RP_V0, _RP_KK, _RP_KA, _RP_RK, _RP_LNG, _RP_LNB = range(8)


def _each(f, *lists):
    return [f(*xs) for xs in zip(*lists)]


def _rw_chunk_local(a_t, r_t, b_h, k_h, v, p_last, strict, incl, eye):
    rnd = lambda xs: _each(lambda z: z.astype(BF16), xs)
    nt = lambda a, b: _mm1(a, b, _NT)
    tn = lambda a, b: _mm1(a, b, _TN)
    at, rt, bh, kh, vs = rnd(a_t), rnd(r_t), rnd(b_h), rnd(k_h), rnd(v)
    m_ab = _each(lambda z: jnp.where(strict, z, 0.0), _each(nt, at, bh))
    m_ak = _each(lambda z: jnp.where(strict, z, 0.0), _each(nt, at, kh))
    m_rb = _each(lambda z: jnp.where(incl, z, 0.0), _each(nt, rt, bh))
    m_rk = _each(lambda z: jnp.where(incl, z, 0.0), _each(nt, rt, kh))
    x = _each(lambda z: eye + z, m_ab)
    pw = m_ab
    for _ in range(max(1, int(math.ceil(math.log2(a_t[0].shape[0]))) - 1)):
        ps = rnd(pw)
        pw = _each(_mm1, ps, ps)
        x = _each(lambda z, d: z + d, x, _each(_mm1, rnd(x), rnd(pw)))
    xs = rnd(x)
    ax = rnd(_each(_mm1, xs, at))
    u0 = rnd(_each(_mm1, xs, rnd(_each(_mm1, rnd(m_ak), vs))))
    rb = rnd(m_rb)
    r_hat = _each(lambda z, d: z + d, r_t, _each(_mm1, rb, ax))
    y0 = _each(lambda z, d: z + d, _each(_mm1, rb, u0), _each(_mm1, rnd(m_rk), vs))
    t_m = _each(lambda z, p: (eye + z) * p, _each(tn, ax, bh), p_last)
    s_add = _each(lambda z, d, p: (z + d) * p, _each(tn, u0, bh), _each(tn, vs, kh), p_last)
    return r_hat, y0, t_m, s_add


def _rw_rec_kernel(*refs, has_vres, n_chunks):
    if has_vres:
        (r_ref, k_ref, v_ref, vf_ref, hw_ref, ha_ref, hg_ref, hv_ref, w2_ref, a2_ref, g2_ref, v2_ref,
         rp_ref, o_ref, s_ref, at_s, rt_s, bh_s, kh_s, v_s, g_s, bon_s, rh_s, y0_s, tm_s, sa_s) = refs
    else:
        (r_ref, k_ref, v_ref, hw_ref, ha_ref, hg_ref, w2_ref, a2_ref, g2_ref,
         rp_ref, o_ref, s_ref, at_s, rt_s, bh_s, kh_s, v_s, g_s, bon_s, rh_s, y0_s, tm_s, sa_s) = refs
    n = RW_HEAD
    c = RW_CHUNK
    heads = RW_LANES // n

    @pl.when(pl.program_id(1) == 0)
    def _():
        s_ref[...] = jnp.zeros_like(s_ref)

    rp = rp_ref[...]
    row = lambda i: rp[i:i + 1, :]
    lane = lax.broadcasted_iota(jnp.int32, (1, RW_LANES), 1)

    def per_head_sum(z):
        out = jnp.zeros_like(z)
        for j in range(heads):
            mine = (lane >= j * n) & (lane < (j + 1) * n)
            out = jnp.where(mine, jnp.sum(jnp.where(mine, z, 0.0), axis=-1, keepdims=True), out)
        return out

    zw = row(_RP_W0) + jnp.dot(hw_ref[...], w2_ref[...], preferred_element_type=F32)
    nz = -zw
    softplus = jnp.maximum(nz, 0.0) + jnp.log1p(jnp.exp(-jnp.abs(nz)))
    lw = -jnp.exp(-softplus - 0.5)
    a = jax.nn.sigmoid(row(_RP_A0) + jnp.dot(ha_ref[...], a2_ref[...], preferred_element_type=F32))
    g_s[...] = jnp.dot(hg_ref[...], g2_ref[...], preferred_element_type=F32)
    r = r_ref[...]
    k = k_ref[...]
    v = v_ref[...]
    if has_vres:
        mixv = jax.nn.sigmoid(row(_RP_V0) + jnp.dot(hv_ref[...], v2_ref[...], preferred_element_type=F32))
        v = v + (vf_ref[...] - v) * mixv
    v_s[...] = v
    kk = k * row(_RP_KK)
    kk = kk * lax.rsqrt(jnp.maximum(per_head_sum(kk * kk), 1e-24))
    k = k * (1.0 + (a - 1.0) * row(_RP_KA))
    bon_s[...] = per_head_sum(r * k * row(_RP_RK)) * v

    ti = lax.broadcasted_iota(jnp.int32, (c, c), 0)
    si = lax.broadcasted_iota(jnp.int32, (c, c), 1)
    strict = si < ti
    incl = si <= ti
    tri_incl = incl.astype(BF16)
    eye = (si == ti).astype(F32)

    p_last = []
    for ci in range(n_chunks):
        rows = slice(ci * c, (ci + 1) * c)
        lw_c = lw[rows]
        cum = sum(jnp.dot(tri_incl, part, preferred_element_type=F32) for part in _split3(lw_c))
        p_in = jnp.exp(cum)
        p_inv = jnp.exp(-cum)
        at_s[rows, :] = -kk[rows] * jnp.exp(cum - lw_c)
        rt_s[rows, :] = r[rows] * p_in
        bh_s[rows, :] = kk[rows] * a[rows] * p_inv
        kh_s[rows, :] = k[rows] * p_inv
        p_last.append(p_in[c - 1:c, :])

    where = [(slice(ci * c, (ci + 1) * c), slice(j * n, (j + 1) * n)) for ci in range(n_chunks) for j in range(heads)]
    r_hat, y0, t_m, s_add = _rw_chunk_local(
        [at_s[w] for w in where], [rt_s[w] for w in where], [bh_s[w] for w in where], [kh_s[w] for w in where],
        [v_s[w] for w in where], [p_last[ci][:, j * n:(j + 1) * n] for ci in range(n_chunks) for j in range(heads)],
        strict, incl, eye)
    for i, w in enumerate(where):
        rh_s[w] = r_hat[i]
        y0_s[w] = y0[i]
        tm_s[i] = t_m[i]
        sa_s[i] = s_add[i]

    s = [s_ref[j] for j in range(heads)]
    for ci in range(n_chunks):
        rows = slice(ci * c, (ci + 1) * c)
        for j in range(heads):
            ln = slice(j * n, (j + 1) * n)
            ss = _split2(s[j])
            y = _mm3(_split2(rh_s[rows, ln]), ss, _NT) + y0_s[rows, ln]
            s[j] = _mm3(ss, _split2(tm_s[ci * heads + j])) + sa_s[ci * heads + j]
            mu = jnp.mean(y, axis=-1, keepdims=True)
            yc = y - mu
            var = jnp.mean(yc * yc, axis=-1, keepdims=True)
            yn = yc * lax.rsqrt(var + RW_GN_EPS) * row(_RP_LNG)[:, ln] + row(_RP_LNB)[:, ln]
            o_ref[rows, ln] = ((yn + bon_s[rows, ln]) * g_s[rows, ln]).astype(o_ref.dtype)
    for j in range(heads):
        s_ref[j] = s[j]


def rw_recurrence(r, k, v, v_first, hw, ha, hg, hv, w2, a2, g2, v2, rowp, *, tb=512):
    t, d = r.shape
    tb = _pick(t, tb)
    has_vres = v_first is not None
    tile = pl.BlockSpec((tb, RW_LANES), lambda hb, i: (i, hb))
    lora = lambda w: pl.BlockSpec((tb, w), lambda hb, i: (i, 0))
    wcol = lambda w: pl.BlockSpec((w, RW_LANES), lambda hb, i: (0, hb))
    lw_, la_, lg_ = hw.shape[1], ha.shape[1], hg.shape[1]
    if has_vres:
        args = [r, k, v, v_first, hw, ha, hg, hv, w2, a2, g2, v2, rowp]
        in_specs = [tile, tile, tile, tile, lora(lw_), lora(la_), lora(lg_), lora(hv.shape[1]),
                    wcol(lw_), wcol(la_), wcol(lg_), wcol(hv.shape[1]), wcol(8)]
    else:
        args = [r, k, v, hw, ha, hg, w2, a2, g2, rowp]
        in_specs = [tile, tile, tile, lora(lw_), lora(la_), lora(lg_), wcol(lw_), wcol(la_), wcol(lg_), wcol(8)]
    heads = RW_LANES // RW_HEAD
    n_chunks = tb // RW_CHUNK
    per_chunk = pltpu.VMEM((n_chunks * heads, RW_HEAD, RW_HEAD), F32)
    return pl.pallas_call(
        functools.partial(_rw_rec_kernel, has_vres=has_vres, n_chunks=n_chunks),
        grid=(d // RW_LANES, t // tb),
        in_specs=in_specs,
        out_specs=tile,
        out_shape=jax.ShapeDtypeStruct((t, d), BF16),
        scratch_shapes=[pltpu.VMEM((heads, RW_HEAD, RW_HEAD), F32)]
        + [pltpu.VMEM((tb, RW_LANES), F32)] * 9 + [per_chunk, per_chunk],
        compiler_params=_cparams(("parallel", "arbitrary")),
        name="rw_recurrence",
    )(*args)


def _pad_cols(w, mult=LANES):
    p = (-w.shape[-1]) % mult
    return jnp.pad(w, ((0, 0), (0, p))) if p else w


def _pad_rows(w, mult=LANES):
    p = (-w.shape[0]) % mult
    return jnp.pad(w, ((0, p), (0, 0))) if p else w


def rwkv7_time_mix(h, v_first, mix, w_rkv, w_o, w0, w1, w2, a0, a1, a2, g1, g2, k_k, k_a, r_k, lnx_g, lnx_b, v_lora):
    t, d = h.shape
    xm = rw_mix(h, mix)
    wb = w_rkv.astype(BF16)
    r = matmul(xm, wb[0], lead=0)
    k = matmul(xm, wb[1], lead=1)
    v = matmul(xm, wb[2], lead=2)
    hw = matmul(xm, _pad_cols(w1).astype(BF16), lead=3, out_dtype=BF16, act=jnp.tanh)
    ha = matmul(xm, _pad_cols(a1).astype(BF16), lead=4, out_dtype=BF16)
    hg = matmul(xm, _pad_cols(g1).astype(BF16), lead=5, out_dtype=BF16, act=jax.nn.sigmoid)
    zero = jnp.zeros((d,), F32)
    rowp = jnp.stack([w0, a0, zero if v_lora is None else v_lora[0], k_k, k_a, r_k.reshape(d), lnx_g, lnx_b])
    if v_lora is None:
        out = rw_recurrence(r, k, v, None, hw, ha, hg, None, _pad_rows(w2).astype(BF16),
                            _pad_rows(a2).astype(BF16), _pad_rows(g2).astype(BF16), None, rowp)
        v_first = v
    else:
        hv = matmul(xm, _pad_cols(v_lora[1]).astype(BF16), lead=2, out_dtype=BF16)
        out = rw_recurrence(r, k, v, v_first, hw, ha, hg, hv, _pad_rows(w2).astype(BF16),
                            _pad_rows(a2).astype(BF16), _pad_rows(g2).astype(BF16),
                            _pad_rows(v_lora[2]).astype(BF16), rowp)
    return out, v_first


def _nsa_kernel(q_ref, gate_ref, kc_ref, vc_ref, ks_ref, vs_ref, *rest, n_heads_total):
    kw_refs = rest[0:5]
    vw_refs = rest[5:10]
    o_ref, m_s, l_s, acc_s, oc_s, imp_s = rest[10:16]
    g = pl.program_id(0)
    qb = pl.program_id(1)
    hpg, dk, qn = NSA_HPG, NSA_DK, Q_BLOCK
    t0 = qb * qn
    hrows = [slice(h * qn, (h + 1) * qn) for h in range(hpg)]

    q2 = q_ref[...]
    qs = jnp.concatenate([q2[:, h * dk:(h + 1) * dk] for h in range(hpg)], axis=0)
    tq = t0 + lax.broadcasted_iota(jnp.int32, (qn, 1), 0)
    slopes = [jnp.exp2(-8.0 * jnp.full((1, 1), g * hpg + h + 1, jnp.int32).astype(F32) / n_heads_total)
              for h in range(hpg)]


    n_cp = kc_ref.shape[1]
    n_sel = (n_cp * CMP_STRIDE) // SEL_BLOCK
    ratio = SEL_BLOCK // CMP_STRIDE
    n_ov = CMP_BLOCK // CMP_STRIDE

    def compressed(ncols):
        kc = kc_ref[0, :ncols, :]
        s_all = lax.dot_general(qs, kc, _NT, preferred_element_type=F32)
        pos_c = lax.broadcasted_iota(jnp.int32, (1, ncols), 1) * CMP_STRIDE + (CMP_BLOCK - 1)
        d_c = (tq - pos_c).astype(F32)
        hide_c = jnp.where(d_c >= 0.0, 0.0, NEG_INF)
        p_grp = jnp.zeros((qn, ncols), F32)
        p_heads = []
        for h in range(hpg):
            s = s_all[hrows[h]] + (hide_c - slopes[h] * d_c)
            m = jnp.max(s, axis=-1, keepdims=True)
            p = jnp.exp(s - m)
            l = jnp.sum(p, axis=-1, keepdims=True)
            p = p * jnp.where(m > 0.5 * NEG_INF, 1.0 / l, 0.0)
            p_grp = p_grp + p
            p_heads.append(p.astype(BF16))
        oc_s[...] = jnp.dot(jnp.concatenate(p_heads, axis=0), vc_ref[0, :ncols, :], preferred_element_type=F32)
        ji = lax.broadcasted_iota(jnp.int32, (n_sel, ncols), 0)
        ci = lax.broadcasted_iota(jnp.int32, (n_sel, ncols), 1)
        off = ci - ratio * ji
        cnt = jnp.maximum(jnp.minimum(jnp.minimum(off + n_ov, ratio - off), jnp.minimum(n_ov, ratio)), 0)
        cnt = cnt.astype(F32).astype(BF16)
        imp_s[...] = sum(lax.dot_general(cnt, part, _NT, preferred_element_type=F32) for part in _split3(p_grp))

    step_c = min(CMP_COLS, n_cp)
    n_var = n_cp // step_c
    need = jnp.minimum(((t0 + qn) // CMP_STRIDE + step_c - 1) // step_c, n_var)
    for b in range(1, n_var + 1):
        @pl.when(need == b)
        def _(b=b):
            compressed(b * step_c)

    tq_l = t0 + lax.broadcasted_iota(jnp.int32, (1, qn), 1)
    cur = tq_l // SEL_BLOCK
    sj = lax.broadcasted_iota(jnp.int32, (n_sel, qn), 0)
    forced = (sj == 0) | (sj == cur) | (sj == cur - 1)
    work = jnp.where(sj > cur, -1.0, jnp.where(forced, FORCE_SCORE, imp_s[...]))
    sel = jnp.zeros((n_sel, qn), F32)
    sjf = sj.astype(F32)
    for _ in range(min(SEL_TOP_N, n_sel)):
        mx = jnp.max(work, axis=0, keepdims=True)
        first = jnp.min(jnp.where(work == mx, sjf, float(n_sel)), axis=0, keepdims=True)
        pick = sjf == first
        sel = jnp.where(pick, 1.0, sel)
        work = jnp.where(pick, -2.0, work)
    sel_b = sel.astype(BF16)

    m_s[...] = jnp.full_like(m_s, NEG_INF)
    l_s[...] = jnp.zeros_like(l_s)
    acc_s[...] = jnp.zeros_like(acc_s)
    n_keys = ks_ref.shape[0]
    tk = min(SEL_TILE, n_keys)
    bpt = tk // SEL_BLOCK

    def sweep(i, diagonal):
        k0 = pl.multiple_of(i * tk, tk)
        kt = ks_ref[pl.ds(k0, tk), :]
        vt = vs_ref[pl.ds(k0, tk), :]
        st = lax.dot_general(qs, kt, _NT, preferred_element_type=F32)
        bi = lax.broadcasted_iota(jnp.int32, (n_sel, tk), 0)
        ki = lax.broadcasted_iota(jnp.int32, (n_sel, tk), 1)
        expand = (bi == i * bpt + ki // SEL_BLOCK).astype(F32).astype(BF16)
        chosen = lax.dot_general(sel_b, expand, _TN, preferred_element_type=F32)
        d_s = (tq - (k0 + lax.broadcasted_iota(jnp.int32, (1, tk), 1))).astype(F32)
        hide = (chosen - 1.0) * (-NEG_INF)
        if diagonal:
            hide = jnp.where(d_s >= 0.0, hide, NEG_INF)
        p_heads, alphas = [], []
        for h in range(hpg):
            s = st[hrows[h]] + (hide - slopes[h] * d_s)
            m_old = m_s[hrows[h], :]
            m_new = jnp.maximum(m_old, jnp.max(s, axis=-1, keepdims=True))
            alpha = jnp.exp(m_old - m_new)
            p = jnp.exp(s - m_new)
            l_s[hrows[h], :] = alpha * l_s[hrows[h], :] + jnp.sum(p, axis=-1, keepdims=True)
            m_s[hrows[h], :] = m_new
            alphas.append(alpha)
            p_heads.append(p.astype(BF16))
        pv = jnp.dot(jnp.concatenate(p_heads, axis=0), vt, preferred_element_type=F32)
        acc_s[...] = jnp.concatenate(alphas, axis=0) * acc_s[...] + pv

    n_below = t0 // tk
    block_used = jnp.max(sel, axis=1, keepdims=True)
    block_id = lax.broadcasted_iota(jnp.int32, (n_sel, 1), 0)

    def below(i, carry):
        in_tile = (block_id >= i * bpt) & (block_id < (i + 1) * bpt)
        used = jnp.max(jnp.where(in_tile, block_used, 0.0))

        @pl.when(used > 0.5)
        def _():
            sweep(i, False)

        return carry

    lax.fori_loop(0, n_below, below, 0)
    sweep(n_below, True)
    o_s = acc_s[...] * (1.0 / l_s[...])

    kw = jnp.concatenate([r[...] for r in kw_refs], axis=0)
    vw = jnp.concatenate([r[...] for r in vw_refs], axis=0)
    sw = lax.dot_general(qs, kw, _NT, preferred_element_type=F32)
    pos_w = t0 - WINDOW + lax.broadcasted_iota(jnp.int32, (1, WINDOW + qn), 1)
    d_wi = tq - pos_w
    d_w = d_wi.astype(F32)
    hide_w = jnp.where((d_wi >= 0) & (d_wi < WINDOW) & (pos_w >= 0), 0.0, NEG_INF)
    p_heads, inv_l = [], []
    for h in range(hpg):
        s = sw[hrows[h]] + (hide_w - slopes[h] * d_w)
        p = jnp.exp(s - jnp.max(s, axis=-1, keepdims=True))
        inv_l.append(1.0 / jnp.sum(p, axis=-1, keepdims=True))
        p_heads.append(p.astype(BF16))
    o_w = jnp.dot(jnp.concatenate(p_heads, axis=0), vw, preferred_element_type=F32) * jnp.concatenate(inv_l, axis=0)

    gate = gate_ref[...]
    o_c = oc_s[...]
    for h in range(hpg):
        g_c, g_s, g_w = (gate[:, b * hpg + h:b * hpg + h + 1] for b in range(3))
        o = g_c * o_c[hrows[h]] + g_s * o_s[hrows[h]] + g_w * o_w[hrows[h]]
        o_ref[:, h * dk:(h + 1) * dk] = o.astype(o_ref.dtype)


def nsa_attention_core(q, gate, k_cmp, v_cmp, kvb, n_groups):
    t, d = q.shape
    gq = NSA_HPG * NSA_DK
    n_qb = t // Q_BLOCK
    nwb = WINDOW // Q_BLOCK
    n_cp = k_cmp.shape[1]
    full = lambda j: pl.BlockSpec((t, NSA_DK), lambda g, qb, j=j: (0, j * n_groups + g))

    def win(j, i):
        return pl.BlockSpec((Q_BLOCK, NSA_DK), lambda g, qb, j=j, i=i: (jnp.maximum(qb - nwb + i, 0), j * n_groups + g))

    in_specs = [pl.BlockSpec((Q_BLOCK, gq), lambda g, qb: (qb, g)),
                pl.BlockSpec((Q_BLOCK, LANES), lambda g, qb: (qb, g)),
                pl.BlockSpec((1, n_cp, NSA_DK), lambda g, qb: (g, 0, 0)),
                pl.BlockSpec((1, n_cp, NSA_DK), lambda g, qb: (g, 0, 0)),
                full(2), full(3)]
    in_specs += [win(4, i) for i in range(nwb + 1)] + [win(5, i) for i in range(nwb + 1)]
    rows = NSA_HPG * Q_BLOCK
    return pl.pallas_call(
        functools.partial(_nsa_kernel, n_heads_total=n_groups * NSA_HPG),
        grid=(n_groups, n_qb),
        in_specs=in_specs,
        out_specs=pl.BlockSpec((Q_BLOCK, gq), lambda g, qb: (qb, g)),
        out_shape=jax.ShapeDtypeStruct((t, d), BF16),
        scratch_shapes=[pltpu.VMEM((rows, 1), F32), pltpu.VMEM((rows, 1), F32),
                        pltpu.VMEM((rows, NSA_DK), F32), pltpu.VMEM((rows, NSA_DK), F32),
                        pltpu.VMEM((t // SEL_BLOCK, Q_BLOCK), F32)],
        compiler_params=_cparams(("parallel", "arbitrary")),
        name="nsa_attention",
    )(q, gate, k_cmp, v_cmp, kvb, kvb, *([kvb] * (2 * (nwb + 1))))


def nsa_shared_kv(hb, w_kv, cmp_pos, cmp_w1, cmp_b1, cmp_w2):
    t = hb.shape[0]
    n_groups = w_kv.shape[1] // (6 * NSA_DK)
    kv = matmul(hb, w_kv.astype(BF16))
    n_chunk = t // CMP_STRIDE
    cmp = []
    for j in range(2):
        z = kv[:, j * n_groups * NSA_DK:(j + 1) * n_groups * NSA_DK]
        ch = z.reshape(n_chunk, CMP_STRIDE, n_groups, NSA_DK).transpose(2, 0, 1, 3).reshape(n_groups, n_chunk, CMP_STRIDE * NSA_DK)
        nxt = jnp.concatenate([ch[:, 1:], jnp.zeros_like(ch[:, :1])], axis=1)
        flat = jnp.concatenate([ch, nxt], axis=-1).reshape(n_groups * n_chunk, CMP_BLOCK * NSA_DK)
        hid = matmul(flat, cmp_w1[j].astype(BF16), out_dtype=BF16, act=jax.nn.silu,
                     a_bias=cmp_pos[j].reshape(-1), bias=cmp_b1[j])
        out = matmul(hid, cmp_w2[j].astype(BF16))
        cmp.append(out.reshape(n_groups, n_chunk, NSA_DK).astype(BF16))
    return cmp[0], cmp[1], kv.astype(BF16)


def nsa_layer(hb, shared, w_qg, b_g):
    k_cmp, v_cmp, kvb = shared
    n_groups = k_cmp.shape[0]
    n_heads = n_groups * NSA_HPG
    dq = n_heads * NSA_DK
    q = matmul(hb, w_qg[:, :dq].astype(BF16), out_dtype=BF16, act=lambda z: z * (NSA_DK ** -0.5))
    pad = LANES - 3 * NSA_HPG
    w_g = w_qg[:, dq:].reshape(-1, n_groups, NSA_HPG, 3).transpose(0, 1, 3, 2).reshape(-1, n_groups, 3 * NSA_HPG)
    w_g = jnp.pad(w_g, ((0, 0), (0, 0), (0, pad))).reshape(-1, n_groups * LANES)
    b_gp = b_g.reshape(n_groups, NSA_HPG, 3).transpose(0, 2, 1).reshape(n_groups, 3 * NSA_HPG)
    b_gp = jnp.pad(b_gp, ((0, 0), (0, pad))).reshape(-1)
    gate = matmul(hb, w_g.astype(BF16), act=jax.nn.sigmoid, bias=b_gp)
    return nsa_attention_core(q, gate, k_cmp, v_cmp, kvb, n_groups)


def kernel(x, p, ln_g, ln_b, ffn1_w13, ffn1_w2, ffn2_w13, ffn2_w2, ple_w, ple_gate_down, ple_gate_up, rw_mix, rw_w_rkv, rw_w_o, rw_w0, rw_w1, rw_w2, rw_a0, rw_a1, rw_a2, rw_g1, rw_g2, rw_k_k, rw_k_a, rw_r_k, rw_lnx_g, rw_lnx_b, rw_v0, rw_v1, rw_v2, nsa_w_kv, cmp_pos, cmp_w1, cmp_b1, cmp_w2, nsa_w_qg, nsa_b_g, nsa_w_o):
    batch, t, d = x.shape
    depth = ln_g.shape[0]
    n_a = rw_mix.shape[0]
    outs = []
    for bi in range(batch):
        h = x[bi]
        hb = h.astype(BF16)
        v_first = None
        shared = None
        for i in range(depth):
            mid = swiglu_up(hb, ffn1_w13[i].astype(BF16))
            h, hb = mm_res_ln(mid, ffn1_w2[i].astype(BF16), h, ln_g[i, 0], ln_b[i, 0], scale=0.5)
            if i < n_a:
                v_lora = None if i == 0 else (rw_v0[i - 1], rw_v1[i - 1], rw_v2[i - 1])
                mix_pre, v_first = rwkv7_time_mix(
                    h, v_first, rw_mix[i], rw_w_rkv[i], rw_w_o[i], rw_w0[i], rw_w1[i], rw_w2[i], rw_a0[i],
                    rw_a1[i], rw_a2[i], rw_g1[i], rw_g2[i], rw_k_k[i], rw_k_a[i], rw_r_k[i], rw_lnx_g[i],
                    rw_lnx_b[i], v_lora)
                w_o = rw_w_o[i]
            else:
                j = i - n_a
                mix_pre = nsa_layer(hb, shared, nsa_w_qg[j], nsa_b_g[j])
                w_o = nsa_w_o[j]
            h, hb = mm_res_ln(mix_pre, w_o.astype(BF16), h, ln_g[i, 1], ln_b[i, 1], scale=1.0)
            mid = swiglu_up(hb, ffn2_w13[i].astype(BF16))
            h, hb = mm_res_ln(mid, ffn2_w2[i].astype(BF16), h, ln_g[i, 2], ln_b[i, 2], scale=0.5)
            tdown = matmul(hb, ple_gate_down[i].astype(BF16), out_dtype=BF16)
            h, hb = ple_ln(tdown, p[i, bi], ple_gate_up[i].astype(BF16), ple_w[i].astype(BF16), h,
                           ln_g[i, 3], ln_b[i, 3])
            if i == n_a - 1:
                shared = nsa_shared_kv(hb, nsa_w_kv, cmp_pos, cmp_w1, cmp_b1, cmp_w2)
        outs.append(h)
    return jnp.stack(outs, axis=0)
```

```python
import functools
import math

import jax
import jax.numpy as jnp
from jax import lax
from jax.experimental import pallas as pl
from jax.experimental.pallas import tpu as pltpu

F32 = jnp.float32
BF16 = jnp.bfloat16

V7X_VMEM_BYTES = 64 * 1024 * 1024
VMEM_LIMIT = V7X_VMEM_BYTES - 8 * 1024 * 1024
LANES = 128
SUBLANES = 8

MM_TM_BYTES = 8 * 1024 * 1024
MM_TN = 512

LN_EPS = 1e-5
LN_SLAB = 128
DEPTH_TOTAL = 4
ALPHA = (2.0 * DEPTH_TOTAL) ** 0.25

RW_HEAD = 64
RW_GN_EPS = 64e-5
RW_CHUNK = 64
RW_LANES = 256

NSA_DK = 128
NSA_HPG = 8
CMP_BLOCK = 32
CMP_STRIDE = 16
SEL_BLOCK = 64
SEL_TOP_N = 16
WINDOW = 512
Q_BLOCK = 128
FORCE_SCORE = 1e4
NEG_INF = -1e30
SEL_TILE = 512
CMP_COLS = 256

_NN = (((1,), (0,)), ((), ()))
_NT = (((1,), (1,)), ((), ()))
_TN = (((0,), (0,)), ((), ()))


def _cparams(sem):
    return pltpu.CompilerParams(dimension_semantics=sem, vmem_limit_bytes=VMEM_LIMIT)


def _pick(n, pref):
    if n <= pref:
        return n
    t = pref
    while n % t:
        t //= 2
    return t


def _mm_kernel(*refs, act, has_abias, has_bias):
    a_ref, b_ref = refs[0], refs[1]
    i = 2
    abias_ref = bias_ref = None
    if has_abias:
        abias_ref = refs[i]; i += 1
    if has_bias:
        bias_ref = refs[i]; i += 1
    o_ref = refs[i]
    a = a_ref[...]
    if has_abias:
        a = a.astype(F32) + abias_ref[...]
    z = jnp.dot(a.astype(BF16), b_ref[...], preferred_element_type=F32)
    if has_bias:
        z = z + bias_ref[...]
    if act is not None:
        z = act(z)
    o_ref[...] = z.astype(o_ref.dtype)


def _whole_k_tiles(m, n, kd, a_itemsize):
    assert kd * MM_TN * 2 * 2 <= VMEM_LIMIT // 4, kd
    return _pick(m, MM_TM_BYTES // (kd * a_itemsize)), _pick(n, MM_TN)


def matmul(a, b, *, lead=None, out_dtype=F32, act=None, a_bias=None, bias=None):
    m, kd = a.shape[-2:]
    _, n = b.shape
    tm, tn = _whole_k_tiles(m, n, kd, a.dtype.itemsize)
    if lead is None:
        a_spec = pl.BlockSpec((tm, kd), lambda i, j: (i, 0))
    else:
        a_spec = pl.BlockSpec((None, tm, kd), lambda i, j: (lead, i, 0))
    in_specs = [a_spec, pl.BlockSpec((kd, tn), lambda i, j: (0, j))]
    args = [a, b]
    if a_bias is not None:
        in_specs.append(pl.BlockSpec((1, kd), lambda i, j: (0, 0)))
        args.append(a_bias.reshape(1, kd).astype(F32))
    if bias is not None:
        in_specs.append(pl.BlockSpec((1, tn), lambda i, j: (0, j)))
        args.append(bias.reshape(1, n).astype(F32))
    return pl.pallas_call(
        functools.partial(_mm_kernel, act=act, has_abias=a_bias is not None, has_bias=bias is not None),
        grid=(m // tm, n // tn),
        in_specs=in_specs,
        out_specs=pl.BlockSpec((tm, tn), lambda i, j: (i, j)),
        out_shape=jax.ShapeDtypeStruct((m, n), out_dtype),
        compiler_params=_cparams(("parallel", "parallel")),
        name="matmul",
    )(*args)


def _swiglu_kernel(a_ref, b1_ref, b3_ref, o_ref):
    a = a_ref[...].astype(BF16)
    g = jnp.dot(a, b1_ref[...], preferred_element_type=F32)
    u = jnp.dot(a, b3_ref[...], preferred_element_type=F32)
    o_ref[...] = (g * jax.nn.sigmoid(g) * u).astype(o_ref.dtype)


def swiglu_up(a, w13):
    m, kd = a.shape
    f = w13.shape[1] // 2
    tm, tn = _whole_k_tiles(m, f, kd, a.dtype.itemsize)
    nj = f // tn
    return pl.pallas_call(
        _swiglu_kernel,
        grid=(m // tm, nj),
        in_specs=[pl.BlockSpec((tm, kd), lambda i, j: (i, 0)),
                  pl.BlockSpec((kd, tn), lambda i, j: (0, j)),
                  pl.BlockSpec((kd, tn), lambda i, j: (0, j + nj))],
        out_specs=pl.BlockSpec((tm, tn), lambda i, j: (i, j)),
        out_shape=jax.ShapeDtypeStruct((m, f), BF16),
        compiler_params=_cparams(("parallel", "parallel")),
        name="swiglu_up",
    )(a, w13, w13)


def _deepnorm(z, g, b):
    mu = jnp.mean(z, axis=-1, keepdims=True)
    zc = z - mu
    var = jnp.mean(zc * zc, axis=-1, keepdims=True)
    return zc * lax.rsqrt(var + LN_EPS) * g + b


def _mm_res_ln_kernel(a_ref, b_ref, h_ref, g_ref, beta_ref, o_ref, ob_ref, *, nj, tn, scale):
    j = pl.program_id(1)
    cols = pl.ds(pl.multiple_of(j * tn, tn), tn)
    o_ref[:, cols] = ALPHA * h_ref[...] + scale * jnp.dot(a_ref[...].astype(BF16), b_ref[...],
                                                           preferred_element_type=F32)

    @pl.when(j == nj - 1)
    def _():
        def slab(i, carry):
            rows = pl.ds(pl.multiple_of(i * LN_SLAB, LN_SLAB), LN_SLAB)
            y = _deepnorm(o_ref[rows, :], g_ref[...], beta_ref[...])
            o_ref[rows, :] = y
            ob_ref[rows, :] = y.astype(BF16)
            return carry

        lax.fori_loop(0, o_ref.shape[0] // LN_SLAB, slab, 0)


def mm_res_ln(a, b, h, g, beta, *, scale, tm=512):
    m, kd = a.shape
    n = b.shape[1]
    tm, tn = _pick(m, tm), _pick(n, MM_TN)
    assert tm % LN_SLAB == 0, (tm, LN_SLAB)
    nj = n // tn
    return pl.pallas_call(
        functools.partial(_mm_res_ln_kernel, nj=nj, tn=tn, scale=scale),
        grid=(m // tm, nj),
        in_specs=[pl.BlockSpec((tm, kd), lambda i, j: (i, 0)),
                  pl.BlockSpec((kd, tn), lambda i, j: (0, j)),
                  pl.BlockSpec((tm, tn), lambda i, j: (i, j)),
                  pl.BlockSpec((1, n), lambda i, j: (0, 0)),
                  pl.BlockSpec((1, n), lambda i, j: (0, 0))],
        out_specs=[pl.BlockSpec((tm, n), lambda i, j: (i, 0)),
                   pl.BlockSpec((tm, n), lambda i, j: (i, 0))],
        out_shape=[jax.ShapeDtypeStruct((m, n), F32), jax.ShapeDtypeStruct((m, n), BF16)],
        compiler_params=_cparams(("parallel", "arbitrary")),
        name="mm_res_ln",
    )(a, b, h, g.reshape(1, n), beta.reshape(1, n))


def _ple_kernel(t_ref, p_ref, gu_ref, pw_ref, h_ref, g_ref, beta_ref, o_ref, ob_ref):
    gate = jax.nn.sigmoid(jnp.dot(t_ref[...], gu_ref[...], preferred_element_type=F32))
    emb = jnp.dot(p_ref[...].astype(BF16), pw_ref[...], preferred_element_type=F32)
    y = _deepnorm(ALPHA * h_ref[...] + gate * emb, g_ref[...], beta_ref[...])
    o_ref[...] = y
    ob_ref[...] = y.astype(BF16)


def ple_ln(t, p, gu, pw, h, g, beta, *, tm=256):
    m, n = h.shape
    e = t.shape[1]
    tm = _pick(m, tm)
    row = lambda i: (i, 0)
    fixed = lambda i: (0, 0)
    return pl.pallas_call(
        _ple_kernel,
        grid=(m // tm,),
        in_specs=[pl.BlockSpec((tm, e), row), pl.BlockSpec((tm, e), row),
                  pl.BlockSpec((e, n), fixed), pl.BlockSpec((e, n), fixed),
                  pl.BlockSpec((tm, n), row), pl.BlockSpec((1, n), fixed), pl.BlockSpec((1, n), fixed)],
        out_specs=[pl.BlockSpec((tm, n), row), pl.BlockSpec((tm, n), row)],
        out_shape=[jax.ShapeDtypeStruct((m, n), F32), jax.ShapeDtypeStruct((m, n), BF16)],
        compiler_params=_cparams(("parallel",)),
        name="ple_ln",
    )(t, p, gu, pw, h, g.reshape(1, n), beta.reshape(1, n))


def _split2(x):
    hi = x.astype(BF16)
    return hi, (x - hi.astype(F32)).astype(BF16)


def _mm3(a, b, dims=_NN):
    dg = lambda x, y: lax.dot_general(x, y, dims, preferred_element_type=F32)
    return dg(a[0], b[0]) + (dg(a[0], b[1]) + dg(a[1], b[0]))


def _mm1(a, b, dims=_NN):
    return lax.dot_general(a, b, dims, preferred_element_type=F32)


def _split3(x):
    hi = x.astype(BF16)
    r1 = x - hi.astype(F32)
    mid = r1.astype(BF16)
    return hi, mid, (r1 - mid.astype(F32)).astype(BF16)


def _rw_mix_kernel(h_ref, above_ref, mix_ref, o_ref):
    x = h_ref[...]
    first = jnp.where(pl.program_id(0) > 0, above_ref[SUBLANES - 1:SUBLANES, :], 0.0)
    row = lax.broadcasted_iota(jnp.int32, x.shape, 0)
    xx = jnp.where(row == 0, first, pltpu.roll(x, 1, axis=0)) - x
    for c in range(6):
        o_ref[c] = (x + xx * mix_ref[c:c + 1, :]).astype(BF16)


def rw_mix(h, mix, *, tm=256, tn=1024):
    m, n = h.shape
    tm, tn = _pick(m, tm), _pick(n, tn)
    per = tm // SUBLANES
    return pl.pallas_call(
        _rw_mix_kernel,
        grid=(m // tm, n // tn),
        in_specs=[pl.BlockSpec((tm, tn), lambda i, j: (i, j)),
                  pl.BlockSpec((SUBLANES, tn), lambda i, j: (jnp.maximum(i * per - 1, 0), j)),
                  pl.BlockSpec((SUBLANES, tn), lambda i, j: (0, j))],
        out_specs=pl.BlockSpec((6, tm, tn), lambda i, j: (0, i, j)),
        out_shape=jax.ShapeDtypeStruct((6, m, n), BF16),
        compiler_params=_cparams(("parallel", "parallel")),
        name="rw_mix",
    )(h, h, jnp.pad(mix, ((0, SUBLANES - mix.shape[0]), (0, 0))))


_RP_W0, _RP_A0, _RP_V0, _RP_KK, _RP_KA, _RP_RK, _RP_LNG, _RP_LNB = range(8)


def _each(f, *lists):
    return [f(*xs) for xs in zip(*lists)]


def _rw_chunk_local(a_t, r_t, b_h, k_h, v, p_last, strict, incl, eye):
    rnd = lambda xs: _each(lambda z: z.astype(BF16), xs)
    nt = lambda a, b: _mm1(a, b, _NT)
    tn = lambda a, b: _mm1(a, b, _TN)
    at, rt, bh, kh, vs = rnd(a_t), rnd(r_t), rnd(b_h), rnd(k_h), rnd(v)
    m_ab = _each(lambda z: jnp.where(strict, z, 0.0), _each(nt, at, bh))
    m_ak = _each(lambda z: jnp.where(strict, z, 0.0), _each(nt, at, kh))
    m_rb = _each(lambda z: jnp.where(incl, z, 0.0), _each(nt, rt, bh))
    m_rk = _each(lambda z: jnp.where(incl, z, 0.0), _each(nt, rt, kh))
    x = _each(lambda z: eye + z, m_ab)
    pw = m_ab
    for _ in range(max(1, int(math.ceil(math.log2(a_t[0].shape[0]))) - 1)):
        ps = rnd(pw)
        pw = _each(_mm1, ps, ps)
        x = _each(lambda z, d: z + d, x, _each(_mm1, rnd(x), rnd(pw)))
    xs = rnd(x)
    ax = rnd(_each(_mm1, xs, at))
    u0 = rnd(_each(_mm1, xs, rnd(_each(_mm1, rnd(m_ak), vs))))
    rb = rnd(m_rb)
    r_hat = _each(lambda z, d: z + d, r_t, _each(_mm1, rb, ax))
    y0 = _each(lambda z, d: z + d, _each(_mm1, rb, u0), _each(_mm1, rnd(m_rk), vs))
    t_m = _each(lambda z, p: (eye + z) * p, _each(tn, ax, bh), p_last)
    s_add = _each(lambda z, d, p: (z + d) * p, _each(tn, u0, bh), _each(tn, vs, kh), p_last)
    return r_hat, y0, t_m, s_add


def _rw_rec_kernel(*refs, has_vres, n_chunks):
    if has_vres:
        (r_ref, k_ref, v_ref, vf_ref, hw_ref, ha_ref, hg_ref, hv_ref, w2_ref, a2_ref, g2_ref, v2_ref,
         rp_ref, o_ref, s_ref, at_s, rt_s, bh_s, kh_s, v_s, g_s, bon_s, rh_s, y0_s, tm_s, sa_s) = refs
    else:
        (r_ref, k_ref, v_ref, hw_ref, ha_ref, hg_ref, w2_ref, a2_ref, g2_ref,
         rp_ref, o_ref, s_ref, at_s, rt_s, bh_s, kh_s, v_s, g_s, bon_s, rh_s, y0_s, tm_s, sa_s) = refs
    n = RW_HEAD
    c = RW_CHUNK
    heads = RW_LANES // n

    @pl.when(pl.program_id(1) == 0)
    def _():
        s_ref[...] = jnp.zeros_like(s_ref)

    rp = rp_ref[...]
    row = lambda i: rp[i:i + 1, :]
    lane = lax.broadcasted_iota(jnp.int32, (1, RW_LANES), 1)

    def per_head_sum(z):
        out = jnp.zeros_like(z)
        for j in range(heads):
            mine = (lane >= j * n) & (lane < (j + 1) * n)
            out = jnp.where(mine, jnp.sum(jnp.where(mine, z, 0.0), axis=-1, keepdims=True), out)
        return out

    zw = row(_RP_W0) + jnp.dot(hw_ref[...], w2_ref[...], preferred_element_type=F32)
    nz = -zw
    softplus = jnp.maximum(nz, 0.0) + jnp.log1p(jnp.exp(-jnp.abs(nz)))
    lw = -jnp.exp(-softplus - 0.5)
    a = jax.nn.sigmoid(row(_RP_A0) + jnp.dot(ha_ref[...], a2_ref[...], preferred_element_type=F32))
    g_s[...] = jnp.dot(hg_ref[...], g2_ref[...], preferred_element_type=F32)
    r = r_ref[...]
    k = k_ref[...]
    v = v_ref[...]
    if has_vres:
        mixv = jax.nn.sigmoid(row(_RP_V0) + jnp.dot(hv_ref[...], v2_ref[...], preferred_element_type=F32))
        v = v + (vf_ref[...] - v) * mixv
    v_s[...] = v
    kk = k * row(_RP_KK)
    kk = kk * lax.rsqrt(jnp.maximum(per_head_sum(kk * kk), 1e-24))
    k = k * (1.0 + (a - 1.0) * row(_RP_KA))
    bon_s[...] = per_head_sum(r * k * row(_RP_RK)) * v

    ti = lax.broadcasted_iota(jnp.int32, (c, c), 0)
    si = lax.broadcasted_iota(jnp.int32, (c, c), 1)
    strict = si < ti
    incl = si <= ti
    tri_incl = incl.astype(BF16)
    eye = (si == ti).astype(F32)

    p_last = []
    for ci in range(n_chunks):
        rows = slice(ci * c, (ci + 1) * c)
        lw_c = lw[rows]
        cum = sum(jnp.dot(tri_incl, part, preferred_element_type=F32) for part in _split3(lw_c))
        p_in = jnp.exp(cum)
        p_inv = jnp.exp(-cum)
        at_s[rows, :] = -kk[rows] * jnp.exp(cum - lw_c)
        rt_s[rows, :] = r[rows] * p_in
        bh_s[rows, :] = kk[rows] * a[rows] * p_inv
        kh_s[rows, :] = k[rows] * p_inv
        p_last.append(p_in[c - 1:c, :])

    where = [(slice(ci * c, (ci + 1) * c), slice(j * n, (j + 1) * n)) for ci in range(n_chunks) for j in range(heads)]
    r_hat, y0, t_m, s_add = _rw_chunk_local(
        [at_s[w] for w in where], [rt_s[w] for w in where], [bh_s[w] for w in where], [kh_s[w] for w in where],
        [v_s[w] for w in where], [p_last[ci][:, j * n:(j + 1) * n] for ci in range(n_chunks) for j in range(heads)],
        strict, incl, eye)
    for i, w in enumerate(where):
        rh_s[w] = r_hat[i]
        y0_s[w] = y0[i]
        tm_s[i] = t_m[i]
        sa_s[i] = s_add[i]

    s = [s_ref[j] for j in range(heads)]
    for ci in range(n_chunks):
        rows = slice(ci * c, (ci + 1) * c)
        for j in range(heads):
            ln = slice(j * n, (j + 1) * n)
            ss = _split2(s[j])
            y = _mm3(_split2(rh_s[rows, ln]), ss, _NT) + y0_s[rows, ln]
            s[j] = _mm3(ss, _split2(tm_s[ci * heads + j])) + sa_s[ci * heads + j]
            mu = jnp.mean(y, axis=-1, keepdims=True)
            yc = y - mu
            var = jnp.mean(yc * yc, axis=-1, keepdims=True)
            yn = yc * lax.rsqrt(var + RW_GN_EPS) * row(_RP_LNG)[:, ln] + row(_RP_LNB)[:, ln]
            o_ref[rows, ln] = ((yn + bon_s[rows, ln]) * g_s[rows, ln]).astype(o_ref.dtype)
    for j in range(heads):
        s_ref[j] = s[j]


def rw_recurrence(r, k, v, v_first, hw, ha, hg, hv, w2, a2, g2, v2, rowp, *, tb=512):
    t, d = r.shape
    tb = _pick(t, tb)
    has_vres = v_first is not None
    tile = pl.BlockSpec((tb, RW_LANES), lambda hb, i: (i, hb))
    lora = lambda w: pl.BlockSpec((tb, w), lambda hb, i: (i, 0))
    wcol = lambda w: pl.BlockSpec((w, RW_LANES), lambda hb, i: (0, hb))
    lw_, la_, lg_ = hw.shape[1], ha.shape[1], hg.shape[1]
    if has_vres:
        args = [r, k, v, v_first, hw, ha, hg, hv, w2, a2, g2, v2, rowp]
        in_specs = [tile, tile, tile, tile, lora(lw_), lora(la_), lora(lg_), lora(hv.shape[1]),
                    wcol(lw_), wcol(la_), wcol(lg_), wcol(hv.shape[1]), wcol(8)]
    else:
        args = [r, k, v, hw, ha, hg, w2, a2, g2, rowp]
        in_specs = [tile, tile, tile, lora(lw_), lora(la_), lora(lg_), wcol(lw_), wcol(la_), wcol(lg_), wcol(8)]
    heads = RW_LANES // RW_HEAD
    n_chunks = tb // RW_CHUNK
    per_chunk = pltpu.VMEM((n_chunks * heads, RW_HEAD, RW_HEAD), F32)
    return pl.pallas_call(
        functools.partial(_rw_rec_kernel, has_vres=has_vres, n_chunks=n_chunks),
        grid=(d // RW_LANES, t // tb),
        in_specs=in_specs,
        out_specs=tile,
        out_shape=jax.ShapeDtypeStruct((t, d), BF16),
        scratch_shapes=[pltpu.VMEM((heads, RW_HEAD, RW_HEAD), F32)]
        + [pltpu.VMEM((tb, RW_LANES), F32)] * 9 + [per_chunk, per_chunk],
        compiler_params=_cparams(("parallel", "arbitrary")),
        name="rw_recurrence",
    )(*args)


def _pad_cols(w, mult=LANES):
    p = (-w.shape[-1]) % mult
    return jnp.pad(w, ((0, 0), (0, p))) if p else w


def _pad_rows(w, mult=LANES):
    p = (-w.shape[0]) % mult
    return jnp.pad(w, ((0, p), (0, 0))) if p else w


def rwkv7_time_mix(h, v_first, mix, w_rkv, w_o, w0, w1, w2, a0, a1, a2, g1, g2, k_k, k_a, r_k, lnx_g, lnx_b, v_lora):
    t, d = h.shape
    xm = rw_mix(h, mix)
    wb = w_rkv.astype(BF16)
    r = matmul(xm, wb[0], lead=0)
    k = matmul(xm, wb[1], lead=1)
    v = matmul(xm, wb[2], lead=2)
    hw = matmul(xm, _pad_cols(w1).astype(BF16), lead=3, out_dtype=BF16, act=jnp.tanh)
    ha = matmul(xm, _pad_cols(a1).astype(BF16), lead=4, out_dtype=BF16)
    hg = matmul(xm, _pad_cols(g1).astype(BF16), lead=5, out_dtype=BF16, act=jax.nn.sigmoid)
    zero = jnp.zeros((d,), F32)
    rowp = jnp.stack([w0, a0, zero if v_lora is None else v_lora[0], k_k, k_a, r_k.reshape(d), lnx_g, lnx_b])
    if v_lora is None:
        out = rw_recurrence(r, k, v, None, hw, ha, hg, None, _pad_rows(w2).astype(BF16),
                            _pad_rows(a2).astype(BF16), _pad_rows(g2).astype(BF16), None, rowp)
        v_first = v
    else:
        hv = matmul(xm, _pad_cols(v_lora[1]).astype(BF16), lead=2, out_dtype=BF16)
        out = rw_recurrence(r, k, v, v_first, hw, ha, hg, hv, _pad_rows(w2).astype(BF16),
                            _pad_rows(a2).astype(BF16), _pad_rows(g2).astype(BF16),
                            _pad_rows(v_lora[2]).astype(BF16), rowp)
    return out, v_first


def _nsa_kernel(q_ref, gate_ref, kc_ref, vc_ref, ks_ref, vs_ref, *rest, n_heads_total):
    kw_refs = rest[0:5]
    vw_refs = rest[5:10]
    o_ref, m_s, l_s, acc_s, oc_s, imp_s = rest[10:16]
    g = pl.program_id(0)
    qb = pl.program_id(1)
    hpg, dk, qn = NSA_HPG, NSA_DK, Q_BLOCK
    t0 = qb * qn
    hrows = [slice(h * qn, (h + 1) * qn) for h in range(hpg)]

    q2 = q_ref[...]
    qs = jnp.concatenate([q2[:, h * dk:(h + 1) * dk] for h in range(hpg)], axis=0)
    tq = t0 + lax.broadcasted_iota(jnp.int32, (qn, 1), 0)
    slopes = [jnp.exp2(-8.0 * jnp.full((1, 1), g * hpg + h + 1, jnp.int32).astype(F32) / n_heads_total)
              for h in range(hpg)]


    n_cp = kc_ref.shape[1]
    n_sel = (n_cp * CMP_STRIDE) // SEL_BLOCK
    ratio = SEL_BLOCK // CMP_STRIDE
    n_ov = CMP_BLOCK // CMP_STRIDE

    def compressed(ncols):
        kc = kc_ref[0, :ncols, :]
        s_all = lax.dot_general(qs, kc, _NT, preferred_element_type=F32)
        pos_c = lax.broadcasted_iota(jnp.int32, (1, ncols), 1) * CMP_STRIDE + (CMP_BLOCK - 1)
        d_c = (tq - pos_c).astype(F32)
        hide_c = jnp.where(d_c >= 0.0, 0.0, NEG_INF)
        p_grp = jnp.zeros((qn, ncols), F32)
        p_heads = []
        for h in range(hpg):
            s = s_all[hrows[h]] + (hide_c - slopes[h] * d_c)
            m = jnp.max(s, axis=-1, keepdims=True)
            p = jnp.exp(s - m)
            l = jnp.sum(p, axis=-1, keepdims=True)
            p = p * jnp.where(m > 0.5 * NEG_INF, 1.0 / l, 0.0)
            p_grp = p_grp + p
            p_heads.append(p.astype(BF16))
        oc_s[...] = jnp.dot(jnp.concatenate(p_heads, axis=0), vc_ref[0, :ncols, :], preferred_element_type=F32)
        ji = lax.broadcasted_iota(jnp.int32, (n_sel, ncols), 0)
        ci = lax.broadcasted_iota(jnp.int32, (n_sel, ncols), 1)
        off = ci - ratio * ji
        cnt = jnp.maximum(jnp.minimum(jnp.minimum(off + n_ov, ratio - off), jnp.minimum(n_ov, ratio)), 0)
        cnt = cnt.astype(F32).astype(BF16)
        imp_s[...] = sum(lax.dot_general(cnt, part, _NT, preferred_element_type=F32) for part in _split3(p_grp))

    step_c = min(CMP_COLS, n_cp)
    n_var = n_cp // step_c
    need = jnp.minimum(((t0 + qn) // CMP_STRIDE + step_c - 1) // step_c, n_var)
    for b in range(1, n_var + 1):
        @pl.when(need == b)
        def _(b=b):
            compressed(b * step_c)

    tq_l = t0 + lax.broadcasted_iota(jnp.int32, (1, qn), 1)
    cur = tq_l // SEL_BLOCK
    sj = lax.broadcasted_iota(jnp.int32, (n_sel, qn), 0)
    forced = (sj == 0) | (sj == cur) | (sj == cur - 1)
    work = jnp.where(sj > cur, -1.0, jnp.where(forced, FORCE_SCORE, imp_s[...]))
    sel = jnp.zeros((n_sel, qn), F32)
    sjf = sj.astype(F32)
    for _ in range(min(SEL_TOP_N, n_sel)):
        mx = jnp.max(work, axis=0, keepdims=True)
        first = jnp.min(jnp.where(work == mx, sjf, float(n_sel)), axis=0, keepdims=True)
        pick = sjf == first
        sel = jnp.where(pick, 1.0, sel)
        work = jnp.where(pick, -2.0, work)
    sel_b = sel.astype(BF16)

    m_s[...] = jnp.full_like(m_s, NEG_INF)
    l_s[...] = jnp.zeros_like(l_s)
    acc_s[...] = jnp.zeros_like(acc_s)
    n_keys = ks_ref.shape[0]
    tk = min(SEL_TILE, n_keys)
    bpt = tk // SEL_BLOCK

    def sweep(i, diagonal):
        k0 = pl.multiple_of(i * tk, tk)
        kt = ks_ref[pl.ds(k0, tk), :]
        vt = vs_ref[pl.ds(k0, tk), :]
        st = lax.dot_general(qs, kt, _NT, preferred_element_type=F32)
        bi = lax.broadcasted_iota(jnp.int32, (n_sel, tk), 0)
        ki = lax.broadcasted_iota(jnp.int32, (n_sel, tk), 1)
        expand = (bi == i * bpt + ki // SEL_BLOCK).astype(F32).astype(BF16)
        chosen = lax.dot_general(sel_b, expand, _TN, preferred_element_type=F32)
        d_s = (tq - (k0 + lax.broadcasted_iota(jnp.int32, (1, tk), 1))).astype(F32)
        hide = (chosen - 1.0) * (-NEG_INF)
        if diagonal:
            hide = jnp.where(d_s >= 0.0, hide, NEG_INF)
        p_heads, alphas = [], []
        for h in range(hpg):
            s = st[hrows[h]] + (hide - slopes[h] * d_s)
            m_old = m_s[hrows[h], :]
            m_new = jnp.maximum(m_old, jnp.max(s, axis=-1, keepdims=True))
            alpha = jnp.exp(m_old - m_new)
            p = jnp.exp(s - m_new)
            l_s[hrows[h], :] = alpha * l_s[hrows[h], :] + jnp.sum(p, axis=-1, keepdims=True)
            m_s[hrows[h], :] = m_new
            alphas.append(alpha)
            p_heads.append(p.astype(BF16))
        pv = jnp.dot(jnp.concatenate(p_heads, axis=0), vt, preferred_element_type=F32)
        acc_s[...] = jnp.concatenate(alphas, axis=0) * acc_s[...] + pv

    n_below = t0 // tk
    block_used = jnp.max(sel, axis=1, keepdims=True)
    block_id = lax.broadcasted_iota(jnp.int32, (n_sel, 1), 0)

    def below(i, carry):
        in_tile = (block_id >= i * bpt) & (block_id < (i + 1) * bpt)
        used = jnp.max(jnp.where(in_tile, block_used, 0.0))

        @pl.when(used > 0.5)
        def _():
            sweep(i, False)

        return carry

    far = jnp.where((block_used > 0.5) & (block_id >= bpt), (block_id // bpt).astype(F32), float(n_sel))
    lo = jnp.minimum(jnp.min(far).astype(jnp.int32), n_below)

    @pl.when(n_below > 0)
    def _():
        sweep(0, False)

    lax.fori_loop(jnp.maximum(lo, 1), n_below, below, 0)
    sweep(n_below, True)
    o_s = acc_s[...] * (1.0 / l_s[...])

    kw = jnp.concatenate([r[...] for r in kw_refs], axis=0)
    vw = jnp.concatenate([r[...] for r in vw_refs], axis=0)
    sw = lax.dot_general(qs, kw, _NT, preferred_element_type=F32)
    pos_w = t0 - WINDOW + lax.broadcasted_iota(jnp.int32, (1, WINDOW + qn), 1)
    d_wi = tq - pos_w
    d_w = d_wi.astype(F32)
    hide_w = jnp.where((d_wi >= 0) & (d_wi < WINDOW) & (pos_w >= 0), 0.0, NEG_INF)
    p_heads, inv_l = [], []
    for h in range(hpg):
        s = sw[hrows[h]] + (hide_w - slopes[h] * d_w)
        p = jnp.exp(s - jnp.max(s, axis=-1, keepdims=True))
        inv_l.append(1.0 / jnp.sum(p, axis=-1, keepdims=True))
        p_heads.append(p.astype(BF16))
    o_w = jnp.dot(jnp.concatenate(p_heads, axis=0), vw, preferred_element_type=F32) * jnp.concatenate(inv_l, axis=0)

    gate = gate_ref[...]
    o_c = oc_s[...]
    for h in range(hpg):
        g_c, g_s, g_w = (gate[:, b * hpg + h:b * hpg + h + 1] for b in range(3))
        o = g_c * o_c[hrows[h]] + g_s * o_s[hrows[h]] + g_w * o_w[hrows[h]]
        o_ref[:, h * dk:(h + 1) * dk] = o.astype(o_ref.dtype)


def nsa_attention_core(q, gate, k_cmp, v_cmp, kvb, n_groups):
    t, d = q.shape
    gq = NSA_HPG * NSA_DK
    n_qb = t // Q_BLOCK
    nwb = WINDOW // Q_BLOCK
    n_cp = k_cmp.shape[1]
    full = lambda j: pl.BlockSpec((t, NSA_DK), lambda g, qb, j=j: (0, j * n_groups + g))

    def win(j, i):
        return pl.BlockSpec((Q_BLOCK, NSA_DK), lambda g, qb, j=j, i=i: (jnp.maximum(qb - nwb + i, 0), j * n_groups + g))

    in_specs = [pl.BlockSpec((Q_BLOCK, gq), lambda g, qb: (qb, g)),
                pl.BlockSpec((Q_BLOCK, LANES), lambda g, qb: (qb, g)),
                pl.BlockSpec((1, n_cp, NSA_DK), lambda g, qb: (g, 0, 0)),
                pl.BlockSpec((1, n_cp, NSA_DK), lambda g, qb: (g, 0, 0)),
                full(2), full(3)]
    in_specs += [win(4, i) for i in range(nwb + 1)] + [win(5, i) for i in range(nwb + 1)]
    rows = NSA_HPG * Q_BLOCK
    return pl.pallas_call(
        functools.partial(_nsa_kernel, n_heads_total=n_groups * NSA_HPG),
        grid=(n_groups, n_qb),
        in_specs=in_specs,
        out_specs=pl.BlockSpec((Q_BLOCK, gq), lambda g, qb: (qb, g)),
        out_shape=jax.ShapeDtypeStruct((t, d), BF16),
        scratch_shapes=[pltpu.VMEM((rows, 1), F32), pltpu.VMEM((rows, 1), F32),
                        pltpu.VMEM((rows, NSA_DK), F32), pltpu.VMEM((rows, NSA_DK), F32),
                        pltpu.VMEM((t // SEL_BLOCK, Q_BLOCK), F32)],
        compiler_params=_cparams(("parallel", "arbitrary")),
        name="nsa_attention",
    )(q, gate, k_cmp, v_cmp, kvb, kvb, *([kvb] * (2 * (nwb + 1))))


def nsa_shared_kv(hb, w_kv, cmp_pos, cmp_w1, cmp_b1, cmp_w2):
    t = hb.shape[0]
    n_groups = w_kv.shape[1] // (6 * NSA_DK)
    kv = matmul(hb, w_kv.astype(BF16))
    n_chunk = t // CMP_STRIDE
    cmp = []
    for j in range(2):
        z = kv[:, j * n_groups * NSA_DK:(j + 1) * n_groups * NSA_DK]
        ch = z.reshape(n_chunk, CMP_STRIDE, n_groups, NSA_DK).transpose(2, 0, 1, 3).reshape(n_groups, n_chunk, CMP_STRIDE * NSA_DK)
        nxt = jnp.concatenate([ch[:, 1:], jnp.zeros_like(ch[:, :1])], axis=1)
        flat = jnp.concatenate([ch, nxt], axis=-1).reshape(n_groups * n_chunk, CMP_BLOCK * NSA_DK)
        hid = matmul(flat, cmp_w1[j].astype(BF16), out_dtype=BF16, act=jax.nn.silu,
                     a_bias=cmp_pos[j].reshape(-1), bias=cmp_b1[j])
        out = matmul(hid, cmp_w2[j].astype(BF16))
        cmp.append(out.reshape(n_groups, n_chunk, NSA_DK).astype(BF16))
    return cmp[0], cmp[1], kv.astype(BF16)


def nsa_layer(hb, shared, w_qg, b_g):
    k_cmp, v_cmp, kvb = shared
    n_groups = k_cmp.shape[0]
    n_heads = n_groups * NSA_HPG
    dq = n_heads * NSA_DK
    q = matmul(hb, w_qg[:, :dq].astype(BF16), out_dtype=BF16, act=lambda z: z * (NSA_DK ** -0.5))
    pad = LANES - 3 * NSA_HPG
    w_g = w_qg[:, dq:].reshape(-1, n_groups, NSA_HPG, 3).transpose(0, 1, 3, 2).reshape(-1, n_groups, 3 * NSA_HPG)
    w_g = jnp.pad(w_g, ((0, 0), (0, 0), (0, pad))).reshape(-1, n_groups * LANES)
    b_gp = b_g.reshape(n_groups, NSA_HPG, 3).transpose(0, 2, 1).reshape(n_groups, 3 * NSA_HPG)
    b_gp = jnp.pad(b_gp, ((0, 0), (0, pad))).reshape(-1)
    gate = matmul(hb, w_g.astype(BF16), act=jax.nn.sigmoid, bias=b_gp)
    return nsa_attention_core(q, gate, k_cmp, v_cmp, kvb, n_groups)


def kernel(x, p, ln_g, ln_b, ffn1_w13, ffn1_w2, ffn2_w13, ffn2_w2, ple_w, ple_gate_down, ple_gate_up, rw_mix, rw_w_rkv, rw_w_o, rw_w0, rw_w1, rw_w2, rw_a0, rw_a1, rw_a2, rw_g1, rw_g2, rw_k_k, rw_k_a, rw_r_k, rw_lnx_g, rw_lnx_b, rw_v0, rw_v1, rw_v2, nsa_w_kv, cmp_pos, cmp_w1, cmp_b1, cmp_w2, nsa_w_qg, nsa_b_g, nsa_w_o):
    batch, t, d = x.shape
    depth = ln_g.shape[0]
    n_a = rw_mix.shape[0]
    outs = []
    for bi in range(batch):
        h = x[bi]
        hb = h.astype(BF16)
        v_first = None
        shared = None
        for i in range(depth):
            mid = swiglu_up(hb, ffn1_w13[i].astype(BF16))
            h, hb = mm_res_ln(mid, ffn1_w2[i].astype(BF16), h, ln_g[i, 0], ln_b[i, 0], scale=0.5)
            if i < n_a:
                v_lora = None if i == 0 else (rw_v0[i - 1], rw_v1[i - 1], rw_v2[i - 1])
                mix_pre, v_first = rwkv7_time_mix(
                    h, v_first, rw_mix[i], rw_w_rkv[i], rw_w_o[i], rw_w0[i], rw_w1[i], rw_w2[i], rw_a0[i],
                    rw_a1[i], rw_a2[i], rw_g1[i], rw_g2[i], rw_k_k[i], rw_k_a[i], rw_r_k[i], rw_lnx_g[i],
                    rw_lnx_b[i], v_lora)
                w_o = rw_w_o[i]
            else:
                j = i - n_a
                mix_pre = nsa_layer(hb, shared, nsa_w_qg[j], nsa_b_g[j])
                w_o = nsa_w_o[j]
            h, hb = mm_res_ln(mix_pre, w_o.astype(BF16), h, ln_g[i, 1], ln_b[i, 1], scale=1.0)
            mid = swiglu_up(hb, ffn2_w13[i].astype(BF16))
            h, hb = mm_res_ln(mid, ffn2_w2[i].astype(BF16), h, ln_g[i, 2], ln_b[i, 2], scale=0.5)
            tdown = matmul(hb, ple_gate_down[i].astype(BF16), out_dtype=BF16)
            h, hb = ple_ln(tdown, p[i, bi], ple_gate_up[i].astype(BF16), ple_w[i].astype(BF16), h,
                           ln_g[i, 3], ln_b[i, 3])
            if i == n_a - 1:
                shared = nsa_shared_kv(hb, nsa_w_kv, cmp_pos, cmp_w1, cmp_b1, cmp_w2)
        outs.append(h)
    return jnp.stack(outs, axis=0)
```
